```python
import jax
import jax.numpy as jnp
from jax import lax
import numpy as np

D_MODEL = 2048
BATCH = 2
SEQ = 16384
DEPTH = 2

GRID_W = 64
CTX_LEN = 256
N_SUB = 3
N_MOD = 3 * N_SUB
D_FF = 5632
FFN_RES = 0.5
NORM_EPS = 1e-6
ROPE_BASE = 10000.0
Q_BLOCK = 128

MLA_HEADS = 4
MLA_Q_LORA = 512
MLA_KV_LORA = 256
MLA_NOPE = 128
MLA_ROPE = 64
MLA_V = 128
MLA_SCALE = (MLA_NOPE + MLA_ROPE) ** -0.5

RG_WIDTH = 512
RG_BLOCKS = 4
RG_BLOCK = RG_WIDTH // RG_BLOCKS
RG_CONV = 4
RG_C = 8.0

DIFF_HEADS = 4
DIFF_QK = 64
DIFF_V = 2 * DIFF_QK
DIFF_SCALE = DIFF_QK ** -0.5

HGRN_HEADS = 4
HGRN_K = 128
HGRN_V = 128
HGRN_CHUNK = 64

N_BRANCH = 4
BRANCH_W = 512
IN_SIZES = (MLA_Q_LORA, MLA_KV_LORA, MLA_ROPE, RG_WIDTH, RG_WIDTH,
            DIFF_HEADS * 2 * DIFF_QK, DIFF_HEADS * 2 * DIFF_QK, DIFF_HEADS * DIFF_V,
            HGRN_HEADS * HGRN_K, HGRN_HEADS * HGRN_K, HGRN_HEADS * HGRN_K,
            HGRN_HEADS * HGRN_V, HGRN_HEADS * HGRN_V, N_BRANCH * D_MODEL)
D_IN = sum(IN_SIZES)
F32 = jnp.float32

kernel_name = 'hybrid_diffusion_parallel_mixer_trunk'


def rmsnorm(x, g):
    xf = x.astype(F32)
    y = xf * lax.rsqrt(jnp.mean(xf * xf, axis=-1, keepdims=True) + NORM_EPS)
    return (y * g.astype(F32)).astype(x.dtype)


def modulated_prenorm(x, mod, j, g):
    return rmsnorm(x, g) * (1.0 + mod[:, 3 * j + 1, None]) + mod[:, 3 * j, None]


def gated_residual(x, y, mod, j, g, weight):
    return x + weight * mod[:, 3 * j + 2, None] * rmsnorm(y, g)


def ffn_sublayer(x, mod, j, pre_g, post_g, w_gate, w_up, w_down):
    h = modulated_prenorm(x, mod, j, pre_g)
    y = (jax.nn.silu(h @ w_gate) * (h @ w_up)) @ w_down
    return gated_residual(x, y, mod, j, post_g, FFN_RES)


def flip(t):
    return jnp.flip(t, axis=1)


def split_combined(u):
    cuts = np.cumsum(IN_SIZES)[:-1].tolist()
    return jnp.split(u, cuts, axis=-1)


def grid_rope_tables(n_tok, rot_dim, dtype):
    n_rows = n_tok // GRID_W
    row = jnp.repeat(jnp.arange(n_rows, dtype=F32), GRID_W)
    col = jnp.tile(jnp.arange(GRID_W, dtype=F32), n_rows)
    axis_dim = rot_dim // 2
    inv_freq = ROPE_BASE ** (-jnp.arange(0, axis_dim, 2, dtype=F32) / axis_dim)
    ang = jnp.concatenate([row[:, None] * inv_freq, col[:, None] * inv_freq], axis=-1)
    return jnp.cos(ang).astype(dtype), jnp.sin(ang).astype(dtype)


def apply_grid_rope(x, cos, sin):
    B, T, H, R = x.shape
    xs = x.reshape(B, T, H, 2, 2, R // 4)
    x1, x2 = xs[..., 0, :], xs[..., 1, :]
    cs = cos.reshape(1, T, 1, 2, R // 4)
    sn = sin.reshape(1, T, 1, 2, R // 4)
    out = jnp.stack([x1 * cs - x2 * sn, x1 * sn + x2 * cs], axis=-2)
    return out.reshape(B, T, H, R)


def attend(q, k, v, scale):
    s = jnp.einsum('bqhd,bkhd->bhqk', q, k).astype(F32) * scale
    p = jax.nn.softmax(s, axis=-1).astype(v.dtype)
    return jnp.einsum('bhqk,bkhd->bqhd', p, v)


def sweep_query_blocks(fn, qs):
    B, T = qs[0].shape[:2]
    nb = T // Q_BLOCK
    blocks = tuple(jnp.swapaxes(q.reshape((B, nb, Q_BLOCK) + q.shape[2:]), 0, 1) for q in qs)
    out = lax.map(lambda qb: fn(*qb), blocks)
    out = jnp.swapaxes(out, 0, 1)
    return out.reshape((B, T) + out.shape[3:])


def mla_queries(uq, q_norm, w_uq, rope):
    B, T, _ = uq.shape
    q = (rmsnorm(uq, q_norm) @ w_uq).reshape(B, T, MLA_HEADS, MLA_NOPE + MLA_ROPE)
    q_nope, q_rope = q[..., :MLA_NOPE], q[..., MLA_NOPE:]
    if rope is not None:
        q_rope = apply_grid_rope(q_rope, *rope)
    return jnp.concatenate([q_nope, q_rope], axis=-1)


def mla_keys_values(ukv, ukr, kv_norm, w_ukv, rope):
    B, T, _ = ukv.shape
    kv = (rmsnorm(ukv, kv_norm) @ w_ukv).reshape(B, T, MLA_HEADS, MLA_NOPE + MLA_V)
    k_nope, v = kv[..., :MLA_NOPE], kv[..., MLA_NOPE:]
    k_rope = ukr[:, :, None, :]
    if rope is not None:
        k_rope = apply_grid_rope(k_rope, *rope)
    k = jnp.concatenate([k_nope, jnp.broadcast_to(k_rope, (B, T, MLA_HEADS, MLA_ROPE))], axis=-1)
    return k, v


def mla_mixer(lat, ctx, q_norm, w_uq, kv_norm, w_ukv, rope, need_ctx):
    uq, ukv, ukr = lat
    cq, ckv, ckr = ctx
    k_c, v_c = mla_keys_values(ckv, ckr, kv_norm, w_ukv, None)
    k_l, v_l = mla_keys_values(ukv, ukr, kv_norm, w_ukv, rope)
    k_all = jnp.concatenate([k_c, k_l], axis=1)
    v_all = jnp.concatenate([v_c, v_l], axis=1)
    q_l = mla_queries(uq, q_norm, w_uq, rope)
    y = sweep_query_blocks(lambda qb: attend(qb, k_all, v_all, MLA_SCALE), (q_l,))
    B, S = y.shape[:2]
    y = y.reshape(B, S, MLA_HEADS * MLA_V)
    yc = None
    if need_ctx:
        q_c = mla_queries(cq, q_norm, w_uq, None)
        yc = attend(q_c, k_c, v_c, MLA_SCALE).reshape(B, ckv.shape[1], MLA_HEADS * MLA_V)
    return y, yc


def centred_depthwise_conv(x, w, b):
    left = RG_CONV // 2
    right = RG_CONV - 1 - left
    y = lax.conv_general_dilated(x, w[:, None, :], window_strides=(1,), padding=[(left, right)],
                                 dimension_numbers=('NWC', 'WIO', 'NWC'),
                                 feature_group_count=x.shape[-1])
    return y + b


def rglru_coeffs(xc, w_r, b_r, w_i, b_i, lam):
    B, T, W = xc.shape
    xb = xc.reshape(B, T, RG_BLOCKS, RG_BLOCK)
    r = jax.nn.sigmoid(jnp.einsum('btnj,njk->btnk', xb, w_r).reshape(B, T, W) + b_r).astype(F32)
    i = jax.nn.sigmoid(jnp.einsum('btnj,njk->btnk', xb, w_i).reshape(B, T, W) + b_i).astype(F32)
    log_a = -RG_C * r * jax.nn.softplus(-lam.astype(F32))
    a = jnp.exp(log_a)
    b = jnp.sqrt(-jnp.expm1(2.0 * log_a)) * (i * xc.astype(F32))
    return a, b


def linear_scan(a, b, h0):
    b = b.at[:, 0].add(a[:, 0] * h0)
    def combine(lhs, rhs):
        return lhs[0] * rhs[0], rhs[0] * lhs[1] + rhs[1]
    _, h = lax.associative_scan(combine, (a, b), axis=1)
    return h


def directional_scan(a, b, h0, reverse):
    if reverse:
        return flip(linear_scan(flip(a), flip(b), h0))
    return linear_scan(a, b, h0)


def rglru_mixer(lat, ctx, conv_w, conv_b, w_r, b_r, w_i, b_i, lam, need_ctx):
    x_l, gate_l = lat
    x_c, gate_c = ctx
    xl = centred_depthwise_conv(x_l, conv_w, conv_b)
    xcv = centred_depthwise_conv(x_c, conv_w, conv_b)
    h0 = jnp.zeros((x_c.shape[0], RG_WIDTH), F32)
    h_lat, h_ctx = [], []
    for d, reverse in enumerate((False, True)):
        a_c, b_c = rglru_coeffs(xcv, w_r[d], b_r[d], w_i[d], b_i[d], lam[d])
        hc = directional_scan(a_c, b_c, h0, reverse)
        h_init = hc[:, 0] if reverse else hc[:, -1]
        a_l, b_l = rglru_coeffs(xl, w_r[d], b_r[d], w_i[d], b_i[d], lam[d])
        h_lat.append(directional_scan(a_l, b_l, h_init, reverse))
        h_ctx.append(hc)
    y = (h_lat[0] + h_lat[1]).astype(x_l.dtype) * jax.nn.gelu(gate_l)
    yc = None
    if need_ctx:
        yc = (h_ctx[0] + h_ctx[1]).astype(x_c.dtype) * jax.nn.gelu(gate_c)
    return y, yc


def diff_split_qk(u, rope):
    B, T, _ = u.shape
    t = u.reshape(B, T, DIFF_HEADS * 2, DIFF_QK)
    if rope is not None:
        t = apply_grid_rope(t, *rope)
    return t.reshape(B, T, DIFF_HEADS, 2, DIFF_QK)


def diff_attend(q, k, v, lam):
    s = jnp.einsum('bqhcd,bkhcd->bchqk', q, k).astype(F32) * DIFF_SCALE
    p = jax.nn.softmax(s, axis=-1)
    a = (p[:, 0] - lam * p[:, 1]).astype(v.dtype)
    return jnp.einsum('bhqk,bkhd->bqhd', a, v)


def diff_mixer(lat, ctx, lam_vecs, subln, lam_init, rope, need_ctx):
    uq, uk, uv = lat
    cq, ck, cv = ctx
    lv = lam_vecs.astype(F32)
    lam = jnp.exp(jnp.sum(lv[0] * lv[1])) - jnp.exp(jnp.sum(lv[2] * lv[3])) + lam_init
    B, S, _ = uq.shape
    Bc, Tc, _ = cq.shape
    k_c = diff_split_qk(ck, None)
    v_c = cv.reshape(Bc, Tc, DIFF_HEADS, DIFF_V)
    k_all = jnp.concatenate([k_c, diff_split_qk(uk, rope)], axis=1)
    v_all = jnp.concatenate([v_c, uv.reshape(B, S, DIFF_HEADS, DIFF_V)], axis=1)
    q_l = diff_split_qk(uq, rope)
    o = sweep_query_blocks(lambda qb: diff_attend(qb, k_all, v_all, lam), (q_l,))
    y = (rmsnorm(o, subln) * (1.0 - lam_init)).reshape(B, S, DIFF_HEADS * DIFF_V)
    yc = None
    if need_ctx:
        oc = diff_attend(diff_split_qk(cq, None), k_c, v_c, lam)
        yc = (rmsnorm(oc, subln) * (1.0 - lam_init)).reshape(Bc, Tc, DIFF_HEADS * DIFF_V)
    return y, yc


def hgrn_query(uq):
    B, T, _ = uq.shape
    return jax.nn.silu(uq).reshape(B, T, HGRN_HEADS, HGRN_K).astype(F32)


def hgrn_forget(uf, lb):
    B, T, _ = uf.shape
    z = uf.reshape(B, T, HGRN_HEADS, HGRN_K).astype(F32)
    lbh = lb.reshape(HGRN_HEADS, HGRN_K)
    f = lbh + (1.0 - lbh) * jax.nn.sigmoid(z)
    return (1.0 - lbh) * jax.nn.sigmoid(-z), jnp.log(f)


def hgrn_values(ui):
    B, T, _ = ui.shape
    return ui.reshape(B, T, HGRN_HEADS, HGRN_V).astype(F32)


def chunk_gla(q, k, v, g, s0):
    B, T, H, K = q.shape
    V = v.shape[-1]
    N, C = T // HGRN_CHUNK, HGRN_CHUNK
    def to_chunks(t):
        return t.reshape(B, N, C, H, t.shape[-1]).transpose(1, 0, 3, 2, 4)
    lower = jnp.tril(jnp.ones((C, C), dtype=bool))[:, :, None]
    def step(S, inp):
        qc, kc, vc, gc = inp
        G = jnp.cumsum(gc, axis=2)
        o_inter = jnp.einsum('bhck,bhkv->bhcv', qc * jnp.exp(G), S)
        rel = jnp.where(lower, G[:, :, :, None, :] - G[:, :, None, :, :], -jnp.inf)
        A = jnp.einsum('bhtk,bhsk,bhtsk->bhts', qc, kc, jnp.exp(rel))
        o_intra = jnp.einsum('bhts,bhsv->bhtv', A, vc)
        G_last = G[:, :, -1]
        S_new = jnp.exp(G_last)[..., None] * S + jnp.einsum(
            'bhck,bhcv->bhkv', kc * jnp.exp(G_last[:, :, None] - G), vc)
        return S_new, o_inter + o_intra
    S_fin, o = lax.scan(step, s0, (to_chunks(q), to_chunks(k), to_chunks(v), to_chunks(g)))
    return o.transpose(1, 0, 3, 2, 4).reshape(B, T, H, V), S_fin


def gla_state(k, v, g):
    G = jnp.cumsum(g, axis=1)
    return jnp.einsum('bthk,bthv->bhkv', k * jnp.exp(G[:, -1:] - G), v)


def hgrn_readout(o, og, g_norm):
    B, T = o.shape[:2]
    return rmsnorm(o, g_norm).reshape(B, T, HGRN_HEADS * HGRN_V).astype(og.dtype) * jax.nn.silu(og)


def hgrn_mixer(lat, ctx, lb, g_norm, need_ctx):
    uq, uff, ufb, ui, og = lat
    cq, cff, cfb, ci, cog = ctx
    kc_f, gc_f = hgrn_forget(cff, lb)
    kc_b, gc_b = hgrn_forget(cfb, lb)
    vc = hgrn_values(ci)
    yc = None
    if need_ctx:
        zero = jnp.zeros((vc.shape[0], HGRN_HEADS, HGRN_K, HGRN_V), F32)
        qc = hgrn_query(cq)
        oc_f, s_f = chunk_gla(qc, kc_f, vc, gc_f, zero)
        oc_b, s_b = chunk_gla(flip(qc), flip(kc_b), flip(vc), flip(gc_b), zero)
        yc = hgrn_readout(oc_f + flip(oc_b), cog, g_norm)
    else:
        s_f = gla_state(kc_f, vc, gc_f)
        s_b = gla_state(flip(kc_b), flip(vc), flip(gc_b))
    q = hgrn_query(uq)
    v = hgrn_values(ui)
    k_f, g_f = hgrn_forget(uff, lb)
    k_b, g_b = hgrn_forget(ufb, lb)
    o_f, _ = chunk_gla(q, k_f, v, g_f, s_f)
    o_b, _ = chunk_gla(flip(q), flip(k_b), flip(v), flip(g_b), s_b)
    return hgrn_readout(o_f + flip(o_b), og, g_norm), yc


def merge_branches(ys, gate_u, merge_b, w_branch, w_out):
    B, T, _ = gate_u.shape
    g = jax.nn.sigmoid(gate_u.reshape(B, T, N_BRANCH, D_MODEL) + merge_b)
    merged = g[:, :, 0] * (ys[0] @ w_branch[0])
    for n in range(1, N_BRANCH):
        merged = merged + g[:, :, n] * (ys[n] @ w_branch[n])
    return merged @ w_out


def setup_inputs(seed: int = 0) -> dict:
    key = jax.random.key(seed)
    keys = jax.random.split(key, 32)
    def nrm(i, shape, scale):
        return jax.random.normal(keys[i], shape, F32) * scale
    def gain(i, shape):
        return 1.0 + nrm(i, shape, 0.02)
    L = DEPTH
    a_target = jax.random.uniform(keys[22], (L, 2, RG_WIDTH), F32, 0.9, 0.999)
    p = a_target ** (1.0 / RG_C)
    return {
        'x': nrm(0, (BATCH, SEQ, D_MODEL), 1.0),
        'c': nrm(1, (BATCH, D_MODEL), 1.0),
        'ctx': nrm(2, (BATCH, CTX_LEN, D_MODEL), 1.0),
        'c_ctx': nrm(3, (D_MODEL,), 1.0),
        'ada_w': nrm(4, (L, D_MODEL, N_MOD * D_MODEL), 0.5 * D_MODEL ** -0.5),
        'ada_b': nrm(5, (L, N_MOD * D_MODEL), 0.02),
        'pre_norm': gain(6, (L, N_SUB, D_MODEL)),
        'post_norm': gain(7, (L, N_SUB, D_MODEL)),
        'ffn_w_gate': nrm(8, (L, 2, D_MODEL, D_FF), D_MODEL ** -0.5),
        'ffn_w_up': nrm(9, (L, 2, D_MODEL, D_FF), D_MODEL ** -0.5),
        'ffn_w_down': nrm(10, (L, 2, D_FF, D_MODEL), D_FF ** -0.5),
        'w_in': nrm(11, (L, D_MODEL, D_IN), D_MODEL ** -0.5),
        'mla_q_norm': gain(12, (L, MLA_Q_LORA)),
        'mla_w_uq': nrm(13, (L, MLA_Q_LORA, MLA_HEADS * (MLA_NOPE + MLA_ROPE)), MLA_Q_LORA ** -0.5),
        'mla_kv_norm': gain(14, (L, MLA_KV_LORA)),
        'mla_w_ukv': nrm(15, (L, MLA_KV_LORA, MLA_HEADS * (MLA_NOPE + MLA_V)), MLA_KV_LORA ** -0.5),
        'rg_conv_w': nrm(16, (L, RG_CONV, RG_WIDTH), RG_CONV ** -0.5),
        'rg_conv_b': nrm(17, (L, RG_WIDTH), 0.02),
        'rg_w_r': nrm(18, (L, 2, RG_BLOCKS, RG_BLOCK, RG_BLOCK), RG_BLOCK ** -0.5),
        'rg_b_r': nrm(19, (L, 2, RG_WIDTH), 0.02),
        'rg_w_i': nrm(20, (L, 2, RG_BLOCKS, RG_BLOCK, RG_BLOCK), RG_BLOCK ** -0.5),
        'rg_b_i': nrm(21, (L, 2, RG_WIDTH), 0.02),
        'rg_lambda': jnp.log(p) - jnp.log1p(-p),
        'diff_lambda': nrm(23, (L, 4, DIFF_QK), 0.1),
        'diff_subln': gain(24, (L, DIFF_V)),
        'hgrn_lb_logits': nrm(25, (L, HGRN_HEADS * HGRN_K), 0.5),
        'hgrn_norm': gain(26, (L, HGRN_V)),
        'merge_b': nrm(27, (L, N_BRANCH, D_MODEL), 0.02),
        'w_branch': nrm(28, (L, N_BRANCH, BRANCH_W, D_MODEL), BRANCH_W ** -0.5),
        'w_out': nrm(29, (L, D_MODEL, D_MODEL), D_MODEL ** -0.5),
    }


def reference(x, c, ctx, c_ctx, ada_w, ada_b, pre_norm, post_norm, ffn_w_gate, ffn_w_up,
              ffn_w_down, w_in, mla_q_norm, mla_w_uq, mla_kv_norm, mla_w_ukv, rg_conv_w,
              rg_conv_b, rg_w_r, rg_b_r, rg_w_i, rg_b_i, rg_lambda, diff_lambda, diff_subln,
              hgrn_lb_logits, hgrn_norm, merge_b, w_branch, w_out):
    B, S, _ = x.shape
    rope_mla = grid_rope_tables(S, MLA_ROPE, x.dtype)
    rope_diff = grid_rope_tables(S, DIFF_QK, x.dtype)
    lb_soft = jax.nn.softmax(hgrn_lb_logits.astype(F32), axis=0)
    hgrn_lb = jnp.cumsum(lb_soft, axis=0) - lb_soft[0]
    cond_lat = jax.nn.silu(c)
    cond_ctx = jax.nn.silu(c_ctx)[None]
    xc = ctx
    for l in range(DEPTH):
        need_ctx = l < DEPTH - 1
        mod = (cond_lat @ ada_w[l] + ada_b[l]).reshape(B, N_MOD, D_MODEL)
        mod_c = (cond_ctx @ ada_w[l] + ada_b[l]).reshape(1, N_MOD, D_MODEL)
        lam_init = 0.8 - 0.6 * float(np.exp(-0.3 * l))

        x = ffn_sublayer(x, mod, 0, pre_norm[l, 0], post_norm[l, 0],
                         ffn_w_gate[l, 0], ffn_w_up[l, 0], ffn_w_down[l, 0])
        xc = ffn_sublayer(xc, mod_c, 0, pre_norm[l, 0], post_norm[l, 0],
                          ffn_w_gate[l, 0], ffn_w_up[l, 0], ffn_w_down[l, 0])

        (a_q, a_kv, a_kr, b_x, b_g, c_q, c_k, c_v, d_q, d_ff, d_fb, d_i, d_g, gts) = split_combined(
            modulated_prenorm(x, mod, 1, pre_norm[l, 1]) @ w_in[l])
        (a_q_c, a_kv_c, a_kr_c, b_x_c, b_g_c, c_q_c, c_k_c, c_v_c, d_q_c, d_ff_c, d_fb_c, d_i_c, d_g_c,
         gts_c) = split_combined(modulated_prenorm(xc, mod_c, 1, pre_norm[l, 1]) @ w_in[l])

        y_a, y_a_c = mla_mixer((a_q, a_kv, a_kr), (a_q_c, a_kv_c, a_kr_c), mla_q_norm[l], mla_w_uq[l],
                               mla_kv_norm[l], mla_w_ukv[l], rope_mla, need_ctx)
        y_b, y_b_c = rglru_mixer((b_x, b_g), (b_x_c, b_g_c), rg_conv_w[l], rg_conv_b[l], rg_w_r[l],
                                 rg_b_r[l], rg_w_i[l], rg_b_i[l], rg_lambda[l], need_ctx)
        y_c, y_c_c = diff_mixer((c_q, c_k, c_v), (c_q_c, c_k_c, c_v_c), diff_lambda[l], diff_subln[l],
                                lam_init, rope_diff, need_ctx)
        y_d, y_d_c = hgrn_mixer((d_q, d_ff, d_fb, d_i, d_g), (d_q_c, d_ff_c, d_fb_c, d_i_c, d_g_c),
                                hgrn_lb[l], hgrn_norm[l], need_ctx)

        mix = merge_branches((y_a, y_b, y_c, y_d), gts, merge_b[l], w_branch[l], w_out[l])
        x = gated_residual(x, mix, mod, 1, post_norm[l, 1], 1.0)
        if need_ctx:
            mix_c = merge_branches((y_a_c, y_b_c, y_c_c, y_d_c), gts_c, merge_b[l], w_branch[l], w_out[l])
            xc = gated_residual(xc, mix_c, mod_c, 1, post_norm[l, 1], 1.0)
            xc = ffn_sublayer(xc, mod_c, 2, pre_norm[l, 2], post_norm[l, 2],
                              ffn_w_gate[l, 1], ffn_w_up[l, 1], ffn_w_down[l, 1])

        x = ffn_sublayer(x, mod, 2, pre_norm[l, 2], post_norm[l, 2],
                         ffn_w_gate[l, 1], ffn_w_up[l, 1], ffn_w_down[l, 1])
    return x
```

```python
import functools

import jax
import jax.numpy as jnp
import numpy as np
from jax import lax
from jax.experimental import pallas as pl
from jax.experimental.pallas import tpu as pltpu

F32 = jnp.float32
BF16 = jnp.bfloat16

D_MODEL = 2048
GRID_W = 64
N_SUB = 3
N_MOD = 3 * N_SUB
FFN_RES = 0.5
NORM_EPS = 1e-6
ROPE_BASE = 10000.0

MLA_HEADS = 4
MLA_Q_LORA = 512
MLA_KV_LORA = 256
MLA_NOPE = 128
MLA_ROPE = 64
MLA_V = 128
MLA_SCALE = (MLA_NOPE + MLA_ROPE) ** -0.5

RG_WIDTH = 512
RG_BLOCKS = 4
RG_BLOCK = RG_WIDTH // RG_BLOCKS
RG_CONV = 4
RG_C = 8.0

DIFF_HEADS = 4
DIFF_QK = 64
DIFF_V = 2 * DIFF_QK
DIFF_SCALE = DIFF_QK ** -0.5

HGRN_HEADS = 4
HGRN_K = 128
HGRN_V = 128

N_BRANCH = 4
BRANCH_W = 512

LANES = 128
SUBLANES = 8
VMEM_LIMIT_BYTES = 56 * 1024 * 1024

NT_DIMS = (((1,), (1,)), ((), ()))


def _params(n_axes):
    return pltpu.CompilerParams(dimension_semantics=("arbitrary",) * n_axes,
                                vmem_limit_bytes=VMEM_LIMIT_BYTES)


def _pick(n, pref):
    t = min(n, pref)
    assert n % t == 0, (n, pref)
    return t


def _rms(x, g):
    return x * lax.rsqrt(jnp.mean(x * x, axis=-1, keepdims=True) + NORM_EPS) * g


def _sigmoid(x):
    return jax.nn.sigmoid(x)


def _dot(a, b):
    return jnp.dot(a, b, preferred_element_type=F32)


def _dot_nt(a, b):
    return lax.dot_general(a, b, NT_DIMS, preferred_element_type=F32)


def _ada_kernel(c_ref, w_ref, b_ref, o_ref):
    c = c_ref[...]
    cond = (c * _sigmoid(c)).astype(BF16)
    o_ref[...] = _dot(cond, w_ref[...].astype(BF16)) + b_ref[...]


def _ada_mod(c8, w, b):
    D, N = w.shape
    tn = _pick(N, 1024)
    return pl.pallas_call(
        _ada_kernel, grid=(N // tn,),
        in_specs=[pl.BlockSpec((SUBLANES, D), lambda n: (0, 0)),
                  pl.BlockSpec((D, tn), lambda n: (0, n)),
                  pl.BlockSpec((1, tn), lambda n: (0, n))],
        out_specs=pl.BlockSpec((SUBLANES, tn), lambda n: (0, n)),
        out_shape=jax.ShapeDtypeStruct((SUBLANES, N), F32),
        compiler_params=_params(1), name="ada_mod")(c8, w, b.reshape(1, N))


def _prenorm(x, mod_ref, j, g):
    return _rms(x, g) * (1.0 + mod_ref[3 * j + 1:3 * j + 2, :]) + mod_ref[3 * j:3 * j + 1, :]


def _ffn_kernel(x_ref, mod_ref, pre_ref, post_ref, wg_ref, wu_ref, wd_ref, o_ref, h_ref, acc_ref,
                *, j, nf):
    f = pl.program_id(2)

    @pl.when(f == 0)
    def _():
        h_ref[...] = _prenorm(x_ref[...], mod_ref, j, pre_ref[...]).astype(BF16)

    h = h_ref[...]
    g = _dot(h, wg_ref[...])
    u = _dot(h, wu_ref[...])
    a = (g * _sigmoid(g) * u).astype(BF16)
    d = _dot(a, wd_ref[...])

    @pl.when(f == 0)
    def _():
        acc_ref[...] = d

    @pl.when(f > 0)
    def _():
        acc_ref[...] += d

    @pl.when(f == nf - 1)
    def _():
        yn = _rms(acc_ref[...], post_ref[...])
        o_ref[...] = x_ref[...] + FFN_RES * mod_ref[3 * j + 2:3 * j + 3, :] * yn


def _ffn(x, mod, j, pre_g, post_g, wg, wu, wd):
    B, T, D = x.shape
    F = wg.shape[1]
    tm = _pick(T, 512)
    tf = _pick(F, 512)
    nf = F // tf
    return pl.pallas_call(
        functools.partial(_ffn_kernel, j=j, nf=nf), grid=(B, T // tm, nf),
        in_specs=[pl.BlockSpec((None, tm, D), lambda b, t, f: (b, t, 0)),
                  pl.BlockSpec((None, N_MOD, D), lambda b, t, f: (b, 0, 0)),
                  pl.BlockSpec((1, D), lambda b, t, f: (0, 0)),
                  pl.BlockSpec((1, D), lambda b, t, f: (0, 0)),
                  pl.BlockSpec((D, tf), lambda b, t, f: (0, f)),
                  pl.BlockSpec((D, tf), lambda b, t, f: (0, f)),
                  pl.BlockSpec((tf, D), lambda b, t, f: (f, 0))],
        out_specs=pl.BlockSpec((None, tm, D), lambda b, t, f: (b, t, 0)),
        out_shape=jax.ShapeDtypeStruct(x.shape, F32),
        scratch_shapes=[pltpu.VMEM((tm, D), BF16), pltpu.VMEM((tm, D), F32)],
        compiler_params=_params(3), name="ffn")(
            x, mod, pre_g.reshape(1, D), post_g.reshape(1, D), wg, wu, wd)


def _inproj_kernel(x_ref, mod_ref, pre_ref, w_ref, o_ref, h_ref):
    @pl.when(pl.program_id(2) == 0)
    def _():
        h_ref[...] = _prenorm(x_ref[...], mod_ref, 1, pre_ref[...]).astype(BF16)

    o_ref[...] = _dot(h_ref[...], w_ref[...])


def _inproj(x, mod, pre_g, w):
    B, T, D = x.shape
    N = w.shape[1]
    tm = _pick(T, 1024)
    tn = _pick(N, 512)
    return pl.pallas_call(
        _inproj_kernel, grid=(B, T // tm, N // tn),
        in_specs=[pl.BlockSpec((None, tm, D), lambda b, t, n: (b, t, 0)),
                  pl.BlockSpec((None, N_MOD, D), lambda b, t, n: (b, 0, 0)),
                  pl.BlockSpec((1, D), lambda b, t, n: (0, 0)),
                  pl.BlockSpec((D, tn), lambda b, t, n: (0, n))],
        out_specs=pl.BlockSpec((None, tm, tn), lambda b, t, n: (b, t, n)),
        out_shape=jax.ShapeDtypeStruct((B, T, N), F32),
        scratch_shapes=[pltpu.VMEM((tm, D), BF16)],
        compiler_params=_params(3), name="inproj")(x, mod, pre_g.reshape(1, D), w)


U_AQ, U_BX, U_BG, U_CQ, U_CK, U_CV, U_DQ, U_DFF, U_DFB, U_DI, U_DG, U_AKV = range(12)
U_BLK = 512


def _rope128(x, c, s):
    lane = lax.broadcasted_iota(jnp.int32, x.shape, 1)
    partner = jnp.where((lane & 31) < 16, pltpu.roll(x, LANES - 16, 1), pltpu.roll(x, 16, 1))
    return x * c + partner * s


def _mla_prep_kernel(uq_ref, ukv_ref, c_ref, s_ref, qn_ref, kvn_ref, wq_ref, wk_ref, wvt_ref,
                     q_ref, k_ref, vt_ref, *, rope):
    uqn = _rms(uq_ref[...], qn_ref[...]).astype(BF16)
    q = _dot(uqn, wq_ref[...]) * MLA_SCALE
    ukv = ukv_ref[...]
    ukvn = _rms(ukv[:, :MLA_KV_LORA], kvn_ref[...]).astype(BF16)
    kn = _dot(ukvn, wk_ref[...])
    kr = ukv[:, MLA_KV_LORA:MLA_KV_LORA + LANES]
    if rope:
        c = c_ref[...]
        s = s_ref[...]
        kr = _rope128(kr, c, s)
    krb = kr.astype(BF16)
    for h in range(MLA_HEADS):
        qr = q[:, 256 * h + 128:256 * h + 256]
        if rope:
            qr = _rope128(qr, c, s)
        q_ref[:, 256 * h:256 * h + 128] = q[:, 256 * h:256 * h + 128].astype(BF16)
        q_ref[:, 256 * h + 128:256 * h + 256] = qr.astype(BF16)
        k_ref[:, 256 * h:256 * h + 128] = kn[:, 128 * h:128 * h + 128].astype(BF16)
        k_ref[:, 256 * h + 128:256 * h + 256] = krb
    vt = _dot_nt(wvt_ref[...], ukvn)
    tm = vt.shape[1]
    vt_ref[...] = vt.reshape(MLA_HEADS, MLA_V, tm).astype(BF16)


def _mla_prep(u, rc, rs, q_norm, kv_norm, wq, wk, wvt, tkc, rope):
    B, T, _ = u.shape
    tm = tkc
    nC = T // tm
    H = MLA_HEADS
    return pl.pallas_call(
        functools.partial(_mla_prep_kernel, rope=rope), grid=(B, nC),
        in_specs=[pl.BlockSpec((None, tm, U_BLK), lambda b, t: (b, t, U_AQ)),
                  pl.BlockSpec((None, tm, U_BLK), lambda b, t: (b, t, U_AKV)),
                  pl.BlockSpec((tm, LANES), lambda b, t: (t, 0)),
                  pl.BlockSpec((tm, LANES), lambda b, t: (t, 0)),
                  pl.BlockSpec((1, MLA_Q_LORA), lambda b, t: (0, 0)),
                  pl.BlockSpec((1, MLA_KV_LORA), lambda b, t: (0, 0)),
                  pl.BlockSpec(wq.shape, lambda b, t: (0, 0)),
                  pl.BlockSpec(wk.shape, lambda b, t: (0, 0)),
                  pl.BlockSpec(wvt.shape, lambda b, t: (0, 0))],
        out_specs=[pl.BlockSpec((None, tm, H * 256), lambda b, t: (b, t, 0)),
                   pl.BlockSpec((None, tm, H * 256), lambda b, t: (b, t, 0)),
                   pl.BlockSpec((None, H, None, MLA_V, tm), lambda b, t: (b, 0, t, 0, 0))],
        out_shape=[jax.ShapeDtypeStruct((B, T, H * 256), BF16),
                   jax.ShapeDtypeStruct((B, T, H * 256), BF16),
                   jax.ShapeDtypeStruct((B, H, nC, MLA_V, tm), BF16)],
        compiler_params=_params(2), name="mla_prep")(
            u, u, rc, rs, q_norm.reshape(1, -1), kv_norm.reshape(1, -1), wq, wk, wvt)


def _softmax_step(k, vt, q, m_ref, l_ref, acc_ref):
    s = _dot_nt(k, q)
    m_prev = m_ref[...]
    m_new = jnp.maximum(m_prev, jnp.max(s, axis=0, keepdims=True))
    alpha = jnp.exp(m_prev - m_new)
    p = jnp.exp(s - m_new)
    l_ref[...] = alpha * l_ref[...] + jnp.sum(p, axis=0, keepdims=True)
    acc_ref[...] = alpha * acc_ref[...] + _dot(vt, p.astype(BF16))
    m_ref[...] = m_new


def _mla_attn_kernel(*refs, n_lat, tkc):
    if n_lat:
        q_ref, kc_ref, vtc_ref, kl_ref, vtl_ref, o_ref, m_ref, l_ref, acc_ref = refs
    else:
        q_ref, kc_ref, vtc_ref, o_ref, m_ref, l_ref, acc_ref = refs
    q = q_ref[...]
    m_ref[...] = jnp.full(m_ref.shape, -jnp.inf, F32)
    l_ref[...] = jnp.zeros(l_ref.shape, F32)
    acc_ref[...] = jnp.zeros(acc_ref.shape, F32)
    _softmax_step(kc_ref[...], vtc_ref[...], q, m_ref, l_ref, acc_ref)
    if n_lat:
        def body(jc, carry):
            r0 = pl.multiple_of(jc * tkc, tkc)
            _softmax_step(kl_ref[pl.ds(r0, tkc), :], vtl_ref[jc], q, m_ref, l_ref, acc_ref)
            return carry
        lax.fori_loop(0, n_lat, body, 0)
    o = acc_ref[...] / l_ref[...]
    o_ref[...] = o.T.astype(o_ref.dtype)


def _mla_attn(q, k_ctx, vt_ctx, k_lat, vt_lat):
    B, Tq, _ = q.shape
    H = MLA_HEADS
    Tc = k_ctx.shape[1]
    tq = _pick(Tq, 512)
    n_lat = 0 if k_lat is None else vt_lat.shape[2]
    tkc = 0 if k_lat is None else vt_lat.shape[4]
    in_specs = [pl.BlockSpec((None, tq, 256), lambda b, h, i: (b, i, h)),
                pl.BlockSpec((None, Tc, 256), lambda b, h, i: (b, 0, h)),
                pl.BlockSpec((None, None, None, MLA_V, Tc), lambda b, h, i: (b, h, 0, 0, 0))]
    args = [q, k_ctx, vt_ctx]
    if n_lat:
        T = k_lat.shape[1]
        in_specs += [pl.BlockSpec((None, T, 256), lambda b, h, i: (b, 0, h)),
                     pl.BlockSpec((None, None, n_lat, MLA_V, tkc), lambda b, h, i: (b, h, 0, 0, 0))]
        args += [k_lat, vt_lat]
    return pl.pallas_call(
        functools.partial(_mla_attn_kernel, n_lat=n_lat, tkc=tkc), grid=(B, H, Tq // tq),
        in_specs=in_specs,
        out_specs=pl.BlockSpec((None, tq, MLA_V), lambda b, h, i: (b, i, h)),
        out_shape=jax.ShapeDtypeStruct((B, Tq, H * MLA_V), BF16),
        scratch_shapes=[pltpu.VMEM((1, tq), F32), pltpu.VMEM((1, tq), F32),
                        pltpu.VMEM((MLA_V, tq), F32)],
        compiler_params=_params(3), name="mla_attn")(*args)


def _diff_prep_kernel(uq_ref, uk_ref, uv_ref, c_ref, s_ref, eye_ref, q_ref, k_ref, vt_ref, *, rope):
    uq = uq_ref[...] * DIFF_SCALE
    uk = uk_ref[...]
    lane = lax.broadcasted_iota(jnp.int32, (uq.shape[0], LANES), 1)
    if rope:
        c = c_ref[...]
        s = s_ref[...]
    for h in range(DIFF_HEADS):
        qh = uq[:, LANES * h:LANES * (h + 1)]
        kh = uk[:, LANES * h:LANES * (h + 1)]
        if rope:
            qh = _rope128(qh, c, s)
            kh = _rope128(kh, c, s)
        q_ref[:, 256 * h:256 * h + 128] = jnp.where(lane < DIFF_QK, qh, 0.0).astype(BF16)
        q_ref[:, 256 * h + 128:256 * h + 256] = jnp.where(lane < DIFF_QK, 0.0, qh).astype(BF16)
        k_ref[:, LANES * h:LANES * (h + 1)] = kh.astype(BF16)
    vt = _dot_nt(eye_ref[...], uv_ref[...].astype(BF16))
    tm = vt.shape[1]
    vt_ref[...] = vt.reshape(DIFF_HEADS, DIFF_V, tm).astype(BF16)


def _diff_prep(u, rc, rs, eye, tkc, rope):
    B, T, _ = u.shape
    tm = tkc
    nC = T // tm
    H = DIFF_HEADS
    return pl.pallas_call(
        functools.partial(_diff_prep_kernel, rope=rope), grid=(B, nC),
        in_specs=[pl.BlockSpec((None, tm, U_BLK), lambda b, t: (b, t, U_CQ)),
                  pl.BlockSpec((None, tm, U_BLK), lambda b, t: (b, t, U_CK)),
                  pl.BlockSpec((None, tm, U_BLK), lambda b, t: (b, t, U_CV)),
                  pl.BlockSpec((tm, LANES), lambda b, t: (t, 0)),
                  pl.BlockSpec((tm, LANES), lambda b, t: (t, 0)),
                  pl.BlockSpec(eye.shape, lambda b, t: (0, 0))],
        out_specs=[pl.BlockSpec((None, tm, H * 256), lambda b, t: (b, t, 0)),
                   pl.BlockSpec((None, tm, H * LANES), lambda b, t: (b, t, 0)),
                   pl.BlockSpec((None, H, None, DIFF_V, tm), lambda b, t: (b, 0, t, 0, 0))],
        out_shape=[jax.ShapeDtypeStruct((B, T, H * 256), BF16),
                   jax.ShapeDtypeStruct((B, T, H * LANES), BF16),
                   jax.ShapeDtypeStruct((B, H, nC, DIFF_V, tm), BF16)],
        compiler_params=_params(2), name="diff_prep")(u, u, u, rc, rs, eye)


def _diff_attn_kernel(*refs, n_lat, tkc, lam_init):
    if n_lat:
        (q_ref, kc_ref, vtc_ref, kl_ref, vtl_ref, lv_ref, sub_ref, o_ref,
         m0, l0, a0, m1, l1, a1) = refs
    else:
        q_ref, kc_ref, vtc_ref, lv_ref, sub_ref, o_ref, m0, l0, a0, m1, l1, a1 = refs
    q0 = q_ref[:, :LANES]
    q1 = q_ref[:, LANES:]
    for m_ref, l_ref, a_ref in ((m0, l0, a0), (m1, l1, a1)):
        m_ref[...] = jnp.full(m_ref.shape, -jnp.inf, F32)
        l_ref[...] = jnp.zeros(l_ref.shape, F32)
        a_ref[...] = jnp.zeros(a_ref.shape, F32)

    def step(k, vt):
        _softmax_step(k, vt, q0, m0, l0, a0)
        _softmax_step(k, vt, q1, m1, l1, a1)

    step(kc_ref[...], vtc_ref[...])
    if n_lat:
        def body(jc, carry):
            r0 = pl.multiple_of(jc * tkc, tkc)
            step(kl_ref[pl.ds(r0, tkc), :], vtl_ref[jc])
            return carry
        lax.fori_loop(0, n_lat, body, 0)
    lv = lv_ref[...]
    lam = (jnp.exp(jnp.sum(lv[0:1] * lv[1:2], axis=-1, keepdims=True))
           - jnp.exp(jnp.sum(lv[2:3] * lv[3:4], axis=-1, keepdims=True)) + lam_init)
    o = a0[...] / l0[...] - lam * (a1[...] / l1[...])
    ot = o.T
    o_ref[...] = (_rms(ot, sub_ref[...]) * (1.0 - lam_init)).astype(o_ref.dtype)


def _diff_attn(q, k_ctx, vt_ctx, k_lat, vt_lat, lam_vecs, subln, lam_init):
    B, Tq, _ = q.shape
    H = DIFF_HEADS
    Tc = k_ctx.shape[1]
    tq = _pick(Tq, 512)
    n_lat = 0 if k_lat is None else vt_lat.shape[2]
    tkc = 0 if k_lat is None else vt_lat.shape[4]
    in_specs = [pl.BlockSpec((None, tq, 256), lambda b, h, i: (b, i, h)),
                pl.BlockSpec((None, Tc, LANES), lambda b, h, i: (b, 0, h)),
                pl.BlockSpec((None, None, None, DIFF_V, Tc), lambda b, h, i: (b, h, 0, 0, 0))]
    args = [q, k_ctx, vt_ctx]
    if n_lat:
        T = k_lat.shape[1]
        in_specs += [pl.BlockSpec((None, T, LANES), lambda b, h, i: (b, 0, h)),
                     pl.BlockSpec((None, None, n_lat, DIFF_V, tkc), lambda b, h, i: (b, h, 0, 0, 0))]
        args += [k_lat, vt_lat]
    in_specs += [pl.BlockSpec((4, DIFF_QK), lambda b, h, i: (0, 0)),
                 pl.BlockSpec((1, DIFF_V), lambda b, h, i: (0, 0))]
    args += [lam_vecs, subln.reshape(1, DIFF_V)]
    scratch = []
    for _ in range(2):
        scratch += [pltpu.VMEM((1, tq), F32), pltpu.VMEM((1, tq), F32), pltpu.VMEM((DIFF_V, tq), F32)]
    return pl.pallas_call(
        functools.partial(_diff_attn_kernel, n_lat=n_lat, tkc=tkc, lam_init=lam_init),
        grid=(B, H, Tq // tq), in_specs=in_specs,
        out_specs=pl.BlockSpec((None, tq, DIFF_V), lambda b, h, i: (b, i, h)),
        out_shape=jax.ShapeDtypeStruct((B, Tq, H * DIFF_V), BF16),
        scratch_shapes=scratch, compiler_params=_params(3), name="diff_attn")(*args)


def _gelu_tanh(x):
    return 0.5 * x * (1.0 + jnp.tanh(np.sqrt(2.0 / np.pi) * (x + 0.044715 * (x * x * x))))


def _rglru_kernel(*refs, rev, tb, nt, final):
    if final:
        (x_ref, xp_ref, xn_ref, cw_ref, cb_ref, wri_ref, bri_ref, lam_ref, h0_ref, gate_ref, hf_ref,
         o_ref, ext_ref, carry_ref) = refs
    else:
        (x_ref, xp_ref, xn_ref, cw_ref, cb_ref, wri_ref, bri_ref, lam_ref, h0_ref,
         o_ref, ext_ref, carry_ref) = refs
    i = pl.program_id(1)
    blk = (nt - 1 - i) if rev else i

    @pl.when(i == 0)
    def _():
        carry_ref[...] = h0_ref[...]

    ext_ref[0:SUBLANES, :] = jnp.where(blk > 0, xp_ref[...], 0.0)
    ext_ref[SUBLANES:SUBLANES + tb, :] = x_ref[...]
    ext_ref[SUBLANES + tb:2 * SUBLANES + tb, :] = jnp.where(blk < nt - 1, xn_ref[...], 0.0)
    left = RG_CONV // 2
    xc = cb_ref[...]
    for tap in range(RG_CONV):
        off = SUBLANES - left + tap
        xc = xc + cw_ref[tap:tap + 1, :] * ext_ref[off:off + tb, :]
    xcb = xc.astype(BF16)

    rs, gs = [], []
    for kb in range(RG_BLOCKS):
        ri = _dot(xcb[:, RG_BLOCK * kb:RG_BLOCK * (kb + 1)], wri_ref[kb])
        rs.append(ri[:, :RG_BLOCK])
        gs.append(ri[:, RG_BLOCK:])
    bri = bri_ref[...]
    r = _sigmoid(jnp.concatenate(rs, axis=1) + bri[0:1])
    ig = _sigmoid(jnp.concatenate(gs, axis=1) + bri[1:2])
    lam = lam_ref[...]
    softplus_neg = jnp.maximum(-lam, 0.0) + jnp.log(1.0 + jnp.exp(-jnp.abs(lam)))
    log_a = -RG_C * r * softplus_neg
    a = jnp.exp(log_a)
    bv = jnp.sqrt(-jnp.tanh(log_a) * (a * a + 1.0)) * (ig * xc)

    row = lax.broadcasted_iota(jnp.int32, a.shape, 0)
    d = 1
    while d < tb:
        if rev:
            keep = row < tb - d
            a_sh = pltpu.roll(a, tb - d, 0)
            b_sh = pltpu.roll(bv, tb - d, 0)
        else:
            keep = row >= d
            a_sh = pltpu.roll(a, d, 0)
            b_sh = pltpu.roll(bv, d, 0)
        bv = a * jnp.where(keep, b_sh, 0.0) + bv
        a = a * jnp.where(keep, a_sh, 1.0)
        d *= 2
    h = a * carry_ref[...] + bv
    last = 0 if rev else tb - 1
    carry_ref[...] = h[last:last + 1, :]
    if final:
        o_ref[...] = ((hf_ref[...] + h) * _gelu_tanh(gate_ref[...])).astype(o_ref.dtype)
    else:
        o_ref[...] = h


def _rglru_pass(u, conv_w, conv_b, wri, bri, lam, h0, hf, rev):
    B, T, _ = u.shape
    W = RG_WIDTH
    tb = _pick(T, 256)
    nt = T // tb
    r8 = tb // SUBLANES
    final = hf is not None

    def tix(i):
        return (nt - 1 - i) if rev else i

    in_specs = [pl.BlockSpec((None, tb, U_BLK), lambda b, i: (b, tix(i), U_BX)),
                pl.BlockSpec((None, SUBLANES, U_BLK),
                             lambda b, i: (b, jnp.maximum(tix(i) * r8 - 1, 0), U_BX)),
                pl.BlockSpec((None, SUBLANES, U_BLK),
                             lambda b, i: (b, jnp.minimum((tix(i) + 1) * r8, T // SUBLANES - 1), U_BX)),
                pl.BlockSpec((RG_CONV, W), lambda b, i: (0, 0)),
                pl.BlockSpec((1, W), lambda b, i: (0, 0)),
                pl.BlockSpec((RG_BLOCKS, RG_BLOCK, 2 * RG_BLOCK), lambda b, i: (0, 0, 0)),
                pl.BlockSpec((2, W), lambda b, i: (0, 0)),
                pl.BlockSpec((1, W), lambda b, i: (0, 0)),
                pl.BlockSpec((None, 1, W), lambda b, i: (b, 0, 0))]
    args = [u, u, u, conv_w, conv_b.reshape(1, W), wri, bri, lam.reshape(1, W), h0]
    if final:
        in_specs += [pl.BlockSpec((None, tb, U_BLK), lambda b, i: (b, tix(i), U_BG)),
                     pl.BlockSpec((None, tb, W), lambda b, i: (b, tix(i), 0))]
        args += [u, hf]
    return pl.pallas_call(
        functools.partial(_rglru_kernel, rev=rev, tb=tb, nt=nt, final=final), grid=(B, nt),
        in_specs=in_specs,
        out_specs=pl.BlockSpec((None, tb, W), lambda b, i: (b, tix(i), 0)),
        out_shape=jax.ShapeDtypeStruct((B, T, W), BF16 if final else F32),
        scratch_shapes=[pltpu.VMEM((tb + 2 * SUBLANES, W), F32), pltpu.VMEM((1, W), F32)],
        compiler_params=_params(2), name="rglru_bwd" if rev else "rglru_fwd")(*args)


HGRN_CHUNK = 128
HGRN_SUB = 8


def _hgrn_chunk(qv, z, v, lb, st, ones, rev):
    C = HGRN_CHUNK
    sg = _sigmoid(z)
    k = (1.0 - lb) * _sigmoid(-z)
    g = jnp.log(lb + (1.0 - lb) * sg)
    row = lax.broadcasted_iota(jnp.int32, (C, LANES), 0)
    col = lax.broadcasted_iota(jnp.int32, (C, C), 1)
    rowc = lax.broadcasted_iota(jnp.int32, (C, C), 0)

    G = g
    d = 1
    while d < C:
        if rev:
            G = G + jnp.where(row < C - d, pltpu.roll(G, C - d, 0), 0.0)
        else:
            G = G + jnp.where(row >= d, pltpu.roll(G, d, 0), 0.0)
        d *= 2
    last = 0 if rev else C - 1
    g_last = G[last:last + 1, :]

    qe = (qv * jnp.exp(G)).astype(BF16)
    o = _dot_nt(qe, st.astype(BF16))
    kdec = (k * jnp.exp(g_last - G)).astype(BF16)
    vb = v.astype(BF16)
    st_new = st * jnp.exp(g_last) + _dot(v.T.astype(BF16), kdec)

    pos = (C - 1 - row) if rev else row
    a_sum = jnp.zeros((C, C), F32)
    hsz = HGRN_SUB
    while hsz < C:
        grp = 2 * hsz
        ng = C // grp
        G3 = G.reshape(ng, grp, LANES)
        ref = G3[:, hsz:hsz + 1, :] if rev else G3[:, hsz - 1:hsz, :]
        ref = jnp.broadcast_to(ref, (ng, grp, LANES)).reshape(C, LANES)
        e = jnp.exp(-jnp.abs(G - ref))
        later = ((pos >> int(np.log2(hsz))) & 1) == 1
        ql = jnp.where(later, qv * e, 0.0).astype(BF16)
        kl = jnp.where(later, 0.0, k * e).astype(BF16)
        al = _dot_nt(ql, kl)
        shift = int(np.log2(grp))
        a_sum = a_sum + jnp.where((rowc >> shift) == (col >> shift), al, 0.0)
        hsz = grp
    o = o + _dot(a_sum.astype(BF16), vb)

    nb = C // HGRN_SUB
    q3 = qv.reshape(nb, HGRN_SUB, LANES)
    k3 = k.reshape(nb, HGRN_SUB, LANES)
    v3 = v.reshape(nb, HGRN_SUB, LANES)
    G3 = G.reshape(nb, HGRN_SUB, LANES)
    rb = lax.broadcasted_iota(jnp.int32, (nb, HGRN_SUB, LANES), 1)
    od = jnp.zeros((nb, HGRN_SUB, LANES), F32)
    for s in range(HGRN_SUB):
        valid = (rb <= s) if rev else (rb >= s)
        e = jnp.exp(jnp.where(valid, G3 - G3[:, s:s + 1, :], 0.0))
        p = jnp.where(valid, q3 * k3[:, s:s + 1, :] * e, 0.0)
        rs = _dot(p.reshape(C, LANES).astype(BF16), ones)
        od = od + rs.reshape(nb, HGRN_SUB, LANES) * v3[:, s:s + 1, :]
    o = o + od.reshape(C, LANES)
    return o, st_new


def _hgrn_kernel(*refs, rev, tb, nt, final):
    if final:
        (q_ref, z_ref, v_ref, lb_ref, s0_ref, ones_ref, of_ref, og_ref, gn_ref,
         o_ref, sfin_ref, st_ref) = refs
    else:
        q_ref, z_ref, v_ref, lb_ref, s0_ref, ones_ref, o_ref, sfin_ref, st_ref = refs
    i = pl.program_id(2)

    @pl.when(i == 0)
    def _():
        st_ref[...] = s0_ref[...]

    lb = lb_ref[...]
    ones = ones_ref[...]
    C = HGRN_CHUNK
    nch = tb // C

    def body(c, carry):
        cc = (nch - 1 - c) if rev else c
        r0 = pl.multiple_of(cc * C, C)
        qin = q_ref[pl.ds(r0, C), :]
        qv = qin * _sigmoid(qin)
        o, st_new = _hgrn_chunk(qv, z_ref[pl.ds(r0, C), :], v_ref[pl.ds(r0, C), :], lb,
                                st_ref[...], ones, rev)
        st_ref[...] = st_new
        if final:
            o = o + of_ref[pl.ds(r0, C), :]
            og = og_ref[pl.ds(r0, C), :]
            o = _rms(o, gn_ref[...]) * (og * _sigmoid(og))
        o_ref[pl.ds(r0, C), :] = o.astype(o_ref.dtype)
        return carry

    lax.fori_loop(0, nch, body, 0)

    @pl.when(i == nt - 1)
    def _():
        sfin_ref[...] = st_ref[...]


def _hgrn_pass(u, z_blk, lb, s0, ones, of, g_norm, rev):
    B, T, _ = u.shape
    H = HGRN_HEADS
    tb = _pick(T, 512)
    nt = T // tb
    final = of is not None

    def tix(i):
        return (nt - 1 - i) if rev else i

    def ublk(blk):
        return pl.BlockSpec((None, tb, LANES), lambda b, h, i: (b, tix(i), blk * (U_BLK // LANES) + h))

    in_specs = [ublk(U_DQ), ublk(z_blk), ublk(U_DI),
                pl.BlockSpec((1, LANES), lambda b, h, i: (0, h)),
                pl.BlockSpec((None, None, HGRN_V, HGRN_K), lambda b, h, i: (b, h, 0, 0)),
                pl.BlockSpec((LANES, LANES), lambda b, h, i: (0, 0))]
    args = [u, u, u, lb.reshape(1, -1), s0, ones]
    if final:
        in_specs += [pl.BlockSpec((None, tb, LANES), lambda b, h, i: (b, tix(i), h)),
                     ublk(U_DG),
                     pl.BlockSpec((1, HGRN_V), lambda b, h, i: (0, 0))]
        args += [of, u, g_norm.reshape(1, HGRN_V)]
    return pl.pallas_call(
        functools.partial(_hgrn_kernel, rev=rev, tb=tb, nt=nt, final=final), grid=(B, H, nt),
        in_specs=in_specs,
        out_specs=[pl.BlockSpec((None, tb, LANES), lambda b, h, i: (b, tix(i), h)),
                   pl.BlockSpec((None, None, HGRN_V, HGRN_K), lambda b, h, i: (b, h, 0, 0))],
        out_shape=[jax.ShapeDtypeStruct((B, T, H * HGRN_V), BF16 if final else F32),
                   jax.ShapeDtypeStruct((B, H, HGRN_V, HGRN_K), F32)],
        scratch_shapes=[pltpu.VMEM((HGRN_V, HGRN_K), F32)],
        compiler_params=_params(3), name="hgrn_bwd" if rev else "hgrn_fwd")(*args)


def _merge_kernel(x_ref, mod_ref, pre_ref, post_ref, ya_ref, yb_ref, yc_ref, yd_ref, w_ref, wb_ref, mb_ref,
                  o_ref, h_ref, acc_ref):
    n = pl.program_id(2)

    @pl.when(n == 0)
    def _():
        h_ref[...] = _prenorm(x_ref[...], mod_ref, 1, pre_ref[...]).astype(BF16)
        acc_ref[...] = jnp.zeros(acc_ref.shape, F32)

    for nb, y_ref in enumerate((ya_ref, yb_ref, yc_ref, yd_ref)):
        @pl.when(n == nb)
        def _(y_ref=y_ref):
            gate = _sigmoid(_dot(h_ref[...], w_ref[...]) + mb_ref[...])
            acc_ref[...] += gate * _dot(y_ref[...], wb_ref[...])

    @pl.when(n == N_BRANCH)
    def _():
        mix = _dot(acc_ref[...].astype(BF16), w_ref[...])
        o_ref[...] = x_ref[...] + mod_ref[5:6, :] * _rms(mix, post_ref[...])


def _merge(x, mod, pre_g, post_g, ys, w5, wb, mb):
    B, T, D = x.shape
    tm = _pick(T, 512)
    last = N_BRANCH - 1
    yspec = pl.BlockSpec((None, tm, BRANCH_W), lambda b, t, n: (b, t, 0))
    return pl.pallas_call(
        _merge_kernel, grid=(B, T // tm, N_BRANCH + 1),
        in_specs=[pl.BlockSpec((None, tm, D), lambda b, t, n: (b, t, 0)),
                  pl.BlockSpec((None, N_MOD, D), lambda b, t, n: (b, 0, 0)),
                  pl.BlockSpec((1, D), lambda b, t, n: (0, 0)),
                  pl.BlockSpec((1, D), lambda b, t, n: (0, 0)),
                  yspec, yspec, yspec, yspec,
                  pl.BlockSpec((None, D, D), lambda b, t, n: (n, 0, 0)),
                  pl.BlockSpec((None, BRANCH_W, D), lambda b, t, n: (jnp.minimum(n, last), 0, 0)),
                  pl.BlockSpec((None, 1, D), lambda b, t, n: (jnp.minimum(n, last), 0, 0))],
        out_specs=pl.BlockSpec((None, tm, D), lambda b, t, n: (b, t, 0)),
        out_shape=jax.ShapeDtypeStruct(x.shape, F32),
        scratch_shapes=[pltpu.VMEM((tm, D), BF16), pltpu.VMEM((tm, D), F32)],
        compiler_params=_params(3), name="merge")(
            x, mod, pre_g.reshape(1, D), post_g.reshape(1, D), *ys, w5, wb, mb)


def _rope_tables(n_tok):
    n_rows = n_tok // GRID_W
    row = jnp.repeat(jnp.arange(n_rows, dtype=F32), GRID_W)
    col = jnp.tile(jnp.arange(GRID_W, dtype=F32), n_rows)
    axis_dim = MLA_ROPE // 2
    inv_freq = ROPE_BASE ** (-jnp.arange(0, axis_dim, 2, dtype=F32) / axis_dim)
    ar = row[:, None] * inv_freq
    ac = col[:, None] * inv_freq
    cos64 = jnp.concatenate([jnp.cos(ar), jnp.cos(ar), jnp.cos(ac), jnp.cos(ac)], axis=-1)
    sin64 = jnp.concatenate([-jnp.sin(ar), jnp.sin(ar), -jnp.sin(ac), jnp.sin(ac)], axis=-1)
    return cos64, sin64


def _layer_weights(l, w_in, mla_w_uq, mla_w_ukv, rg_w_r, rg_w_i, rg_b_r, rg_b_i, w_out):
    D = D_MODEL
    cuts = np.cumsum([MLA_Q_LORA, MLA_KV_LORA, MLA_ROPE, RG_WIDTH, RG_WIDTH, 512, 512, 512,
                      512, 512, 512, 512, 512])
    wi = w_in[l]
    parts = jnp.split(wi[:, :cuts[-1]], cuts[:-1].tolist(), axis=1)
    (a_q, a_kv, a_kr, b_x, b_g, c_q, c_k, c_v, d_q, d_ff, d_fb, d_i, d_g) = parts
    pad = jnp.zeros((D, U_BLK - MLA_KV_LORA - MLA_ROPE), wi.dtype)
    w_u = jnp.concatenate([a_q, b_x, b_g, c_q, c_k, c_v, d_q, d_ff, d_fb, d_i, d_g, a_kv, a_kr, pad],
                          axis=1).astype(BF16)
    w_gates = wi[:, cuts[-1]:].reshape(D, N_BRANCH, D).transpose(1, 0, 2)
    w5 = jnp.concatenate([w_gates, w_out[l][None]], axis=0).astype(BF16)

    wq = mla_w_uq[l].reshape(MLA_Q_LORA, MLA_HEADS, MLA_NOPE + MLA_ROPE)
    wq = jnp.pad(wq, ((0, 0), (0, 0), (0, 256 - MLA_NOPE - MLA_ROPE))).reshape(MLA_Q_LORA, -1).astype(BF16)
    wkv = mla_w_ukv[l].reshape(MLA_KV_LORA, MLA_HEADS, MLA_NOPE + MLA_V)
    wk = wkv[:, :, :MLA_NOPE].reshape(MLA_KV_LORA, -1).astype(BF16)
    wvt = wkv[:, :, MLA_NOPE:].reshape(MLA_KV_LORA, -1).T.astype(BF16)

    wri = [jnp.concatenate([rg_w_r[l, d], rg_w_i[l, d]], axis=-1).astype(BF16) for d in range(2)]
    bri = [jnp.stack([rg_b_r[l, d], rg_b_i[l, d]], axis=0) for d in range(2)]
    return w_u, w5, wq, wk, wvt, wri, bri


def kernel(x, c, ctx, c_ctx, ada_w, ada_b, pre_norm, post_norm, ffn_w_gate, ffn_w_up, ffn_w_down, w_in, mla_q_norm, mla_w_uq, mla_kv_norm, mla_w_ukv, rg_conv_w, rg_conv_b, rg_w_r, rg_b_r, rg_w_i, rg_b_i, rg_lambda, diff_lambda, diff_subln, hgrn_lb_logits, hgrn_norm, merge_b, w_branch, w_out):
    B, S, D = x.shape
    Tc = ctx.shape[1]
    depth = ada_w.shape[0]
    tkc = _pick(S, 512)

    cos64, sin64 = _rope_tables(S)
    one64 = jnp.ones((S, 64), F32)
    zero64 = jnp.zeros((S, 64), F32)
    rc_mla = jnp.concatenate([cos64, one64], axis=1)
    rs_mla = jnp.concatenate([sin64, zero64], axis=1)
    rc_diff = jnp.concatenate([cos64, cos64], axis=1)
    rs_diff = jnp.concatenate([sin64, sin64], axis=1)
    rc_ctx = jnp.ones((Tc, LANES), F32)
    rs_ctx = jnp.zeros((Tc, LANES), F32)

    lb_soft = jax.nn.softmax(hgrn_lb_logits.astype(F32), axis=0)
    hgrn_lb = jnp.cumsum(lb_soft, axis=0) - lb_soft[0]

    c8 = jnp.zeros((SUBLANES, D), F32).at[:B].set(c).at[B].set(c_ctx)
    eye = jnp.eye(DIFF_HEADS * DIFF_V, dtype=BF16)
    ones = jnp.ones((LANES, LANES), BF16)
    zero_state = jnp.zeros((B, HGRN_HEADS, HGRN_V, HGRN_K), F32)
    zero_h = jnp.zeros((B, 1, RG_WIDTH), F32)

    xc = ctx
    for l in range(depth):
        need_ctx = l < depth - 1
        lam_init = 0.8 - 0.6 * float(np.exp(-0.3 * l))
        mod8 = _ada_mod(c8, ada_w[l], ada_b[l]).reshape(SUBLANES, N_MOD, D)
        mod = mod8[:B]
        mod_c = jnp.broadcast_to(mod8[B:B + 1], (B, N_MOD, D))
        w_u, w5, wq, wk, wvt, wri, bri = _layer_weights(
            l, w_in, mla_w_uq, mla_w_ukv, rg_w_r, rg_w_i, rg_b_r, rg_b_i, w_out)
        wg = [ffn_w_gate[l, i].astype(BF16) for i in range(2)]
        wu = [ffn_w_up[l, i].astype(BF16) for i in range(2)]
        wd = [ffn_w_down[l, i].astype(BF16) for i in range(2)]
        wb = w_branch[l].astype(BF16)
        mb = merge_b[l].reshape(N_BRANCH, 1, D)

        x = _ffn(x, mod, 0, pre_norm[l, 0], post_norm[l, 0], wg[0], wu[0], wd[0])
        xc = _ffn(xc, mod_c, 0, pre_norm[l, 0], post_norm[l, 0], wg[0], wu[0], wd[0])

        u = _inproj(x, mod, pre_norm[l, 1], w_u)
        uc = _inproj(xc, mod_c, pre_norm[l, 1], w_u)

        qa_c, ka_c, vta_c = _mla_prep(uc, rc_ctx, rs_ctx, mla_q_norm[l], mla_kv_norm[l], wq, wk, wvt, Tc, False)
        qa, ka, vta = _mla_prep(u, rc_mla, rs_mla, mla_q_norm[l], mla_kv_norm[l], wq, wk, wvt, tkc, True)
        y_a = _mla_attn(qa, ka_c, vta_c, ka, vta)

        qd_c, kd_c, vtd_c = _diff_prep(uc, rc_ctx, rs_ctx, eye, Tc, False)
        qd, kd, vtd = _diff_prep(u, rc_diff, rs_diff, eye, tkc, True)
        y_c = _diff_attn(qd, kd_c, vtd_c, kd, vtd, diff_lambda[l], diff_subln[l], lam_init)

        hc_f = _rglru_pass(uc, rg_conv_w[l], rg_conv_b[l], wri[0], bri[0], rg_lambda[l, 0], zero_h, None, False)
        hl_f = _rglru_pass(u, rg_conv_w[l], rg_conv_b[l], wri[0], bri[0], rg_lambda[l, 0],
                           hc_f[:, Tc - 1:Tc, :], None, False)
        if need_ctx:
            y_b_c = _rglru_pass(uc, rg_conv_w[l], rg_conv_b[l], wri[1], bri[1], rg_lambda[l, 1], zero_h, hc_f, True)
        hc_b = _rglru_pass(uc, rg_conv_w[l], rg_conv_b[l], wri[1], bri[1], rg_lambda[l, 1], zero_h, None, True)
        y_b = _rglru_pass(u, rg_conv_w[l], rg_conv_b[l], wri[1], bri[1], rg_lambda[l, 1],
                          hc_b[:, 0:1, :], hl_f, True)

        oc_f, s_f = _hgrn_pass(uc, U_DFF, hgrn_lb[l], zero_state, ones, None, None, False)
        if need_ctx:
            y_d_c, s_b = _hgrn_pass(uc, U_DFB, hgrn_lb[l], zero_state, ones, oc_f, hgrn_norm[l], True)
        else:
            _, s_b = _hgrn_pass(uc, U_DFB, hgrn_lb[l], zero_state, ones, None, None, True)
        o_f, _ = _hgrn_pass(u, U_DFF, hgrn_lb[l], s_f, ones, None, None, False)
        y_d, _ = _hgrn_pass(u, U_DFB, hgrn_lb[l], s_b, ones, o_f, hgrn_norm[l], True)

        x = _merge(x, mod, pre_norm[l, 1], post_norm[l, 1], (y_a, y_b, y_c, y_d), w5, wb, mb)
        if need_ctx:
            y_a_c = _mla_attn(qa_c, ka_c, vta_c, None, None)
            y_c_c = _diff_attn(qd_c, kd_c, vtd_c, None, None, diff_lambda[l], diff_subln[l], lam_init)
            xc = _merge(xc, mod_c, pre_norm[l, 1], post_norm[l, 1], (y_a_c, y_b_c, y_c_c, y_d_c), w5, wb, mb)
            xc = _ffn(xc, mod_c, 2, pre_norm[l, 2], post_norm[l, 2], wg[1], wu[1], wd[1])

        x = _ffn(x, mod, 2, pre_norm[l, 2], post_norm[l, 2], wg[1], wu[1], wd[1])
    return x
```

```python
import functools
from typing import Any, NamedTuple

import jax
import jax.numpy as jnp
import numpy as np
from jax import lax
from jax.experimental import pallas as pl
from jax.experimental.pallas import tpu as pltpu

F32 = jnp.float32
BF16 = jnp.bfloat16

D_MODEL = 2048
GRID_W = 64
N_SUB = 3
N_MOD = 3 * N_SUB
FFN_RES = 0.5
NORM_EPS = 1e-6
ROPE_BASE = 10000.0

MLA_HEADS = 4
MLA_Q_LORA = 512
MLA_KV_LORA = 256
MLA_NOPE = 128
MLA_ROPE = 64
MLA_V = 128
MLA_SCALE = (MLA_NOPE + MLA_ROPE) ** -0.5

RG_WIDTH = 512
RG_BLOCKS = 4
RG_BLOCK = RG_WIDTH // RG_BLOCKS
RG_CONV = 4
RG_C = 8.0

DIFF_HEADS = 4
DIFF_QK = 64
DIFF_V = 2 * DIFF_QK
DIFF_SCALE = DIFF_QK ** -0.5

HGRN_HEADS = 4
HGRN_K = 128
HGRN_V = 128

N_BRANCH = 4
BRANCH_W = 512

LANES = 128
SUBLANES = 8
VMEM_LIMIT_BYTES = 56 * 1024 * 1024

NT_DIMS = (((1,), (1,)), ((), ()))


def _params(n_axes):
    return pltpu.CompilerParams(dimension_semantics=("arbitrary",) * n_axes,
                                vmem_limit_bytes=VMEM_LIMIT_BYTES)


def _pick(n, pref):
    t = min(n, pref)
    assert n % t == 0, (n, pref)
    return t


def _rms(x, g):
    return x * lax.rsqrt(jnp.mean(x * x, axis=-1, keepdims=True) + NORM_EPS) * g


def _sigmoid(x):
    return jax.nn.sigmoid(x)


def _dot(a, b):
    return jnp.dot(a, b, preferred_element_type=F32)


def _dot_nt(a, b):
    return lax.dot_general(a, b, NT_DIMS, preferred_element_type=F32)


def _ada_kernel(c_ref, w_ref, b_ref, o_ref):
    c = c_ref[...]
    cond = (c * _sigmoid(c)).astype(BF16)
    o_ref[...] = _dot(cond, w_ref[...].astype(BF16)) + b_ref[...]


def _ada_mod(c8, w, b):
    D, N = w.shape
    tn = _pick(N, 1024)
    return pl.pallas_call(
        _ada_kernel, grid=(N // tn,),
        in_specs=[pl.BlockSpec((SUBLANES, D), lambda n: (0, 0)),
                  pl.BlockSpec((D, tn), lambda n: (0, n)),
                  pl.BlockSpec((1, tn), lambda n: (0, n))],
        out_specs=pl.BlockSpec((SUBLANES, tn), lambda n: (0, n)),
        out_shape=jax.ShapeDtypeStruct((SUBLANES, N), F32),
        compiler_params=_params(1), name="ada_mod")(c8, w, b.reshape(1, N))


def _prenorm(x, mod_ref, j, g):
    return _rms(x, g) * (1.0 + mod_ref[3 * j + 1:3 * j + 2, :]) + mod_ref[3 * j:3 * j + 1, :]


def _ffn_kernel(x_ref, mod_ref, pre_ref, post_ref, wg_ref, wu_ref, wd_ref, o_ref, h_ref, acc_ref,
                *, j, nf):
    f = pl.program_id(2)

    @pl.when(f == 0)
    def _():
        h_ref[...] = _prenorm(x_ref[...], mod_ref, j, pre_ref[...]).astype(BF16)
        acc_ref[...] = jnp.zeros(acc_ref.shape, F32)

    h = h_ref[...]
    g = _dot(h, wg_ref[...])
    u = _dot(h, wu_ref[...])
    a = (g * _sigmoid(g) * u).astype(BF16)
    acc_ref[...] += _dot(a, wd_ref[...])

    @pl.when(f == nf - 1)
    def _():
        yn = _rms(acc_ref[...], post_ref[...])
        o_ref[...] = x_ref[...] + FFN_RES * mod_ref[3 * j + 2:3 * j + 3, :] * yn


def _ffn(x, mod, j, pre_g, post_g, wg, wu, wd):
    B, T, D = x.shape
    F = wg.shape[1]
    tm = _pick(T, 512)
    tf = _pick(F, 512)
    nf = F // tf
    return pl.pallas_call(
        functools.partial(_ffn_kernel, j=j, nf=nf), grid=(B, T // tm, nf),
        in_specs=[pl.BlockSpec((None, tm, D), lambda b, t, f: (b, t, 0)),
                  pl.BlockSpec((None, N_MOD, D), lambda b, t, f: (b, 0, 0)),
                  pl.BlockSpec((1, D), lambda b, t, f: (0, 0)),
                  pl.BlockSpec((1, D), lambda b, t, f: (0, 0)),
                  pl.BlockSpec((D, tf), lambda b, t, f: (0, f)),
                  pl.BlockSpec((D, tf), lambda b, t, f: (0, f)),
                  pl.BlockSpec((tf, D), lambda b, t, f: (f, 0))],
        out_specs=pl.BlockSpec((None, tm, D), lambda b, t, f: (b, t, 0)),
        out_shape=jax.ShapeDtypeStruct(x.shape, F32),
        scratch_shapes=[pltpu.VMEM((tm, D), BF16), pltpu.VMEM((tm, D), F32)],
        compiler_params=_params(3), name="ffn")(
            x, mod, pre_g.reshape(1, D), post_g.reshape(1, D), wg, wu, wd)


def _inproj_kernel(x_ref, mod_ref, pre_ref, w_ref, o_ref, h_ref):
    @pl.when(pl.program_id(2) == 0)
    def _():
        h_ref[...] = _prenorm(x_ref[...], mod_ref, 1, pre_ref[...]).astype(BF16)

    o_ref[...] = _dot(h_ref[...], w_ref[...])


def _inproj(x, mod, pre_g, w):
    B, T, D = x.shape
    N = w.shape[1]
    tm = _pick(T, 1024)
    tn = _pick(N, 512)
    return pl.pallas_call(
        _inproj_kernel, grid=(B, T // tm, N // tn),
        in_specs=[pl.BlockSpec((None, tm, D), lambda b, t, n: (b, t, 0)),
                  pl.BlockSpec((None, N_MOD, D), lambda b, t, n: (b, 0, 0)),
                  pl.BlockSpec((1, D), lambda b, t, n: (0, 0)),
                  pl.BlockSpec((D, tn), lambda b, t, n: (0, n))],
        out_specs=pl.BlockSpec((None, tm, tn), lambda b, t, n: (b, t, n)),
        out_shape=jax.ShapeDtypeStruct((B, T, N), F32),
        scratch_shapes=[pltpu.VMEM((tm, D), BF16)],
        compiler_params=_params(3), name="inproj")(x, mod, pre_g.reshape(1, D), w)


U_AQ, U_BX, U_BG, U_CQ, U_CK, U_CV, U_DQ, U_DFF, U_DFB, U_DI, U_DG, U_AKV = range(12)
U_BLK = 512


def _rope128(x, c, s):
    lane = lax.broadcasted_iota(jnp.int32, x.shape, 1)
    partner = jnp.where((lane & 31) < 16, pltpu.roll(x, LANES - 16, 1), pltpu.roll(x, 16, 1))
    return x * c + partner * s


def _mla_prep_kernel(uq_ref, ukv_ref, c_ref, s_ref, qn_ref, kvn_ref, wq_ref, wk_ref, wvt_ref,
                     q_ref, k_ref, vt_ref, *, rope):
    uqn = _rms(uq_ref[...], qn_ref[...]).astype(BF16)
    q = _dot(uqn, wq_ref[...]) * (MLA_SCALE * LOG2E)
    ukv = ukv_ref[...]
    ukvn = _rms(ukv[:, :MLA_KV_LORA], kvn_ref[...]).astype(BF16)
    kn = _dot(ukvn, wk_ref[...])
    kr = ukv[:, MLA_KV_LORA:MLA_KV_LORA + LANES]
    if rope:
        c = c_ref[...]
        s = s_ref[...]
        kr = _rope128(kr, c, s)
    krb = kr.astype(BF16)
    for h in range(MLA_HEADS):
        qr = q[:, 256 * h + 128:256 * h + 256]
        if rope:
            qr = _rope128(qr, c, s)
        q_ref[:, 256 * h:256 * h + 128] = q[:, 256 * h:256 * h + 128].astype(BF16)
        q_ref[:, 256 * h + 128:256 * h + 256] = qr.astype(BF16)
        k_ref[:, 256 * h:256 * h + 128] = kn[:, 128 * h:128 * h + 128].astype(BF16)
        k_ref[:, 256 * h + 128:256 * h + 256] = krb
    vt = _dot_nt(wvt_ref[...], ukvn)
    tm = vt.shape[1]
    vt_ref[...] = vt.reshape(MLA_HEADS, MLA_V, tm).astype(BF16)


def _mla_prep(u, rc, rs, q_norm, kv_norm, wq, wk, wvt, tkc, rope):
    B, T, _ = u.shape
    tm = tkc
    nC = T // tm
    H = MLA_HEADS
    return pl.pallas_call(
        functools.partial(_mla_prep_kernel, rope=rope), grid=(B, nC),
        in_specs=[pl.BlockSpec((None, tm, U_BLK), lambda b, t: (b, t, U_AQ)),
                  pl.BlockSpec((None, tm, U_BLK), lambda b, t: (b, t, U_AKV)),
                  pl.BlockSpec((tm, LANES), lambda b, t: (t, 0)),
                  pl.BlockSpec((tm, LANES), lambda b, t: (t, 0)),
                  pl.BlockSpec((1, MLA_Q_LORA), lambda b, t: (0, 0)),
                  pl.BlockSpec((1, MLA_KV_LORA), lambda b, t: (0, 0)),
                  pl.BlockSpec(wq.shape, lambda b, t: (0, 0)),
                  pl.BlockSpec(wk.shape, lambda b, t: (0, 0)),
                  pl.BlockSpec(wvt.shape, lambda b, t: (0, 0))],
        out_specs=[pl.BlockSpec((None, tm, H * 256), lambda b, t: (b, t, 0)),
                   pl.BlockSpec((None, tm, H * 256), lambda b, t: (b, t, 0)),
                   pl.BlockSpec((None, H, None, MLA_V, tm), lambda b, t: (b, 0, t, 0, 0))],
        out_shape=[jax.ShapeDtypeStruct((B, T, H * 256), BF16),
                   jax.ShapeDtypeStruct((B, T, H * 256), BF16),
                   jax.ShapeDtypeStruct((B, H, nC, MLA_V, tm), BF16)],
        compiler_params=_params(2), name="mla_prep")(
            u, u, rc, rs, q_norm.reshape(1, -1), kv_norm.reshape(1, -1), wq, wk, wvt)


ATTN_TQ = 1024
ATTN_SUB = 512
LOG2E = float(np.log2(np.e))


class _Stream(NamedTuple):
    m: Any
    l: Any
    acc: Any
    s: Any = None
    cm: Any = None
    p: Any = None
    al: Any = None


def _stream_scratch(tq, tkc, dv):
    shapes = [pltpu.VMEM((1, tq), F32), pltpu.VMEM((1, tq), F32), pltpu.VMEM((dv, tq), F32)]
    if tkc:
        shapes += [pltpu.VMEM((2, tkc, tq), F32), pltpu.VMEM((2, 1, tq), F32),
                   pltpu.VMEM((2, tkc, tq), BF16), pltpu.VMEM((2, 1, tq), F32)]
    return shapes


def _softmax_step(k, vt, q, st):
    s = _dot_nt(k, q)
    m_prev = st.m[...]
    m_new = jnp.maximum(m_prev, jnp.max(s, axis=0, keepdims=True))
    alpha = jnp.exp2(m_prev - m_new)
    p = jnp.exp2(s - m_new)
    st.l[...] = alpha * st.l[...] + jnp.sum(p, axis=0, keepdims=True)
    st.acc[...] = alpha * st.acc[...] + _dot(vt, p.astype(BF16))
    st.m[...] = m_new


def _stage_scores(k, qs, streams, slot):
    for q, st in zip(qs, streams):
        s = _dot_nt(k, q)
        st.s[slot] = s
        st.cm[slot] = jnp.max(s, axis=0, keepdims=True)


def _stage_softmax(streams, slot):
    for st in streams:
        m_prev = st.m[...]
        m_new = jnp.maximum(m_prev, st.cm[slot])
        alpha = jnp.exp2(m_prev - m_new)
        p = jnp.exp2(st.s[slot] - m_new)
        st.l[...] = alpha * st.l[...] + jnp.sum(p, axis=0, keepdims=True)
        st.p[slot] = p.astype(BF16)
        st.al[slot] = alpha
        st.m[...] = m_new


def _stage_values(vt, streams, slot):
    for st in streams:
        st.acc[...] = st.al[slot] * st.acc[...] + _dot(vt, st.p[slot])


def _attend(qs, kc, vtc, kl_ref, vtl_ref, n_lat, tkc, streams):
    for st in streams:
        st.m[...] = jnp.full(st.m.shape, -jnp.inf, F32)
        st.l[...] = jnp.zeros(st.l.shape, F32)
        st.acc[...] = jnp.zeros(st.acc.shape, F32)
    for q, st in zip(qs, streams):
        _softmax_step(kc, vtc, q, st)
    if not n_lat:
        return

    def keys(j):
        return kl_ref[pl.ds(pl.multiple_of(j * tkc, tkc), tkc), :]

    if n_lat < 4 or n_lat % 2:
        def plain(j, carry):
            for q, st in zip(qs, streams):
                _softmax_step(keys(j), vtl_ref[j], q, st)
            return carry
        lax.fori_loop(0, n_lat, plain, 0)
        return

    _stage_scores(keys(0), qs, streams, 0)
    _stage_scores(keys(1), qs, streams, 1)
    _stage_softmax(streams, 0)

    def body(i, carry):
        j = 2 * i
        _stage_scores(keys(j), qs, streams, 0)
        _stage_softmax(streams, 1)
        _stage_values(vtl_ref[j - 2], streams, 0)
        _stage_scores(keys(j + 1), qs, streams, 1)
        _stage_softmax(streams, 0)
        _stage_values(vtl_ref[j - 1], streams, 1)
        return carry

    lax.fori_loop(1, n_lat // 2, body, 0)
    _stage_softmax(streams, 1)
    _stage_values(vtl_ref[n_lat - 2], streams, 0)
    _stage_values(vtl_ref[n_lat - 1], streams, 1)


def _split_streams(scratch, n):
    per = len(scratch) // n
    return [_Stream(*scratch[i * per:(i + 1) * per]) for i in range(n)]


def _mla_attn_kernel(*refs, n_lat, tkc, n_sub, ts):
    if n_lat:
        q_ref, kc_ref, vtc_ref, kl_ref, vtl_ref, o_ref = refs[:6]
        scratch = refs[6:]
    else:
        q_ref, kc_ref, vtc_ref, o_ref = refs[:4]
        kl_ref = vtl_ref = None
        scratch = refs[4:]
    streams = _split_streams(scratch, n_sub)
    qs = [q_ref[sub * ts:(sub + 1) * ts, :] for sub in range(n_sub)]
    _attend(qs, kc_ref[...], vtc_ref[...], kl_ref, vtl_ref, n_lat, tkc, streams)
    for sub, st in enumerate(streams):
        o = st.acc[...] / st.l[...]
        o_ref[sub * ts:(sub + 1) * ts, :] = o.T.astype(o_ref.dtype)


def _mla_attn(q, k_ctx, vt_ctx, k_lat, vt_lat):
    B, Tq, _ = q.shape
    H = MLA_HEADS
    Tc = k_ctx.shape[1]
    tq = _pick(Tq, ATTN_TQ)
    n_lat = 0 if k_lat is None else vt_lat.shape[2]
    tkc = 0 if k_lat is None else vt_lat.shape[4]
    in_specs = [pl.BlockSpec((None, tq, 256), lambda b, h, i: (b, i, h)),
                pl.BlockSpec((None, Tc, 256), lambda b, h, i: (b, 0, h)),
                pl.BlockSpec((None, None, None, MLA_V, Tc), lambda b, h, i: (b, h, 0, 0, 0))]
    args = [q, k_ctx, vt_ctx]
    if n_lat:
        T = k_lat.shape[1]
        in_specs += [pl.BlockSpec((None, T, 256), lambda b, h, i: (b, 0, h)),
                     pl.BlockSpec((None, None, n_lat, MLA_V, tkc), lambda b, h, i: (b, h, 0, 0, 0))]
        args += [k_lat, vt_lat]
    ts = _pick(tq, ATTN_SUB)
    n_sub = tq // ts
    return pl.pallas_call(
        functools.partial(_mla_attn_kernel, n_lat=n_lat, tkc=tkc, n_sub=n_sub, ts=ts),
        grid=(B, H, Tq // tq), in_specs=in_specs,
        out_specs=pl.BlockSpec((None, tq, MLA_V), lambda b, h, i: (b, i, h)),
        out_shape=jax.ShapeDtypeStruct((B, Tq, H * MLA_V), BF16),
        scratch_shapes=_stream_scratch(ts, tkc, MLA_V) * n_sub,
        compiler_params=_params(3), name="mla_attn")(*args)


def _diff_prep_kernel(uq_ref, uk_ref, uv_ref, c_ref, s_ref, eye_ref, q_ref, k_ref, vt_ref, *, rope):
    uq = uq_ref[...] * (DIFF_SCALE * LOG2E)
    uk = uk_ref[...]
    lane = lax.broadcasted_iota(jnp.int32, (uq.shape[0], LANES), 1)
    if rope:
        c = c_ref[...]
        s = s_ref[...]
    for h in range(DIFF_HEADS):
        qh = uq[:, LANES * h:LANES * (h + 1)]
        kh = uk[:, LANES * h:LANES * (h + 1)]
        if rope:
            qh = _rope128(qh, c, s)
            kh = _rope128(kh, c, s)
        q_ref[:, 256 * h:256 * h + 128] = jnp.where(lane < DIFF_QK, qh, 0.0).astype(BF16)
        q_ref[:, 256 * h + 128:256 * h + 256] = jnp.where(lane < DIFF_QK, 0.0, qh).astype(BF16)
        k_ref[:, LANES * h:LANES * (h + 1)] = kh.astype(BF16)
    vt = _dot_nt(eye_ref[...], uv_ref[...].astype(BF16))
    tm = vt.shape[1]
    vt_ref[...] = vt.reshape(DIFF_HEADS, DIFF_V, tm).astype(BF16)


def _diff_prep(u, rc, rs, eye, tkc, rope):
    B, T, _ = u.shape
    tm = tkc
    nC = T // tm
    H = DIFF_HEADS
    return pl.pallas_call(
        functools.partial(_diff_prep_kernel, rope=rope), grid=(B, nC),
        in_specs=[pl.BlockSpec((None, tm, U_BLK), lambda b, t: (b, t, U_CQ)),
                  pl.BlockSpec((None, tm, U_BLK), lambda b, t: (b, t, U_CK)),
                  pl.BlockSpec((None, tm, U_BLK), lambda b, t: (b, t, U_CV)),
                  pl.BlockSpec((tm, LANES), lambda b, t: (t, 0)),
                  pl.BlockSpec((tm, LANES), lambda b, t: (t, 0)),
                  pl.BlockSpec(eye.shape, lambda b, t: (0, 0))],
        out_specs=[pl.BlockSpec((None, tm, H * 256), lambda b, t: (b, t, 0)),
                   pl.BlockSpec((None, tm, H * LANES), lambda b, t: (b, t, 0)),
                   pl.BlockSpec((None, H, None, DIFF_V, tm), lambda b, t: (b, 0, t, 0, 0))],
        out_shape=[jax.ShapeDtypeStruct((B, T, H * 256), BF16),
                   jax.ShapeDtypeStruct((B, T, H * LANES), BF16),
                   jax.ShapeDtypeStruct((B, H, nC, DIFF_V, tm), BF16)],
        compiler_params=_params(2), name="diff_prep")(u, u, u, rc, rs, eye)


def _diff_attn_kernel(*refs, n_lat, tkc, lam_init, n_sub, ts):
    if n_lat:
        q_ref, kc_ref, vtc_ref, kl_ref, vtl_ref, lv_ref, sub_ref, o_ref = refs[:8]
        scratch = refs[8:]
    else:
        q_ref, kc_ref, vtc_ref, lv_ref, sub_ref, o_ref = refs[:6]
        kl_ref = vtl_ref = None
        scratch = refs[6:]
    streams = _split_streams(scratch, 2 * n_sub)
    qs = []
    for sub in range(n_sub):
        qs += [q_ref[sub * ts:(sub + 1) * ts, :LANES], q_ref[sub * ts:(sub + 1) * ts, LANES:]]
    _attend(qs, kc_ref[...], vtc_ref[...], kl_ref, vtl_ref, n_lat, tkc, streams)
    lv = lv_ref[...]
    lam = (jnp.exp(jnp.sum(lv[0:1] * lv[1:2], axis=-1, keepdims=True))
           - jnp.exp(jnp.sum(lv[2:3] * lv[3:4], axis=-1, keepdims=True)) + lam_init)
    for sub in range(n_sub):
        st0, st1 = streams[2 * sub], streams[2 * sub + 1]
        o = st0.acc[...] / st0.l[...] - lam * (st1.acc[...] / st1.l[...])
        ot = o.T
        o_ref[sub * ts:(sub + 1) * ts, :] = (_rms(ot, sub_ref[...]) * (1.0 - lam_init)).astype(o_ref.dtype)


def _diff_attn(q, k_ctx, vt_ctx, k_lat, vt_lat, lam_vecs, subln, lam_init):
    B, Tq, _ = q.shape
    H = DIFF_HEADS
    Tc = k_ctx.shape[1]
    tq = _pick(Tq, ATTN_TQ)
    n_lat = 0 if k_lat is None else vt_lat.shape[2]
    tkc = 0 if k_lat is None else vt_lat.shape[4]
    in_specs = [pl.BlockSpec((None, tq, 256), lambda b, h, i: (b, i, h)),
                pl.BlockSpec((None, Tc, LANES), lambda b, h, i: (b, 0, h)),
                pl.BlockSpec((None, None, None, DIFF_V, Tc), lambda b, h, i: (b, h, 0, 0, 0))]
    args = [q, k_ctx, vt_ctx]
    if n_lat:
        T = k_lat.shape[1]
        in_specs += [pl.BlockSpec((None, T, LANES), lambda b, h, i: (b, 0, h)),
                     pl.BlockSpec((None, None, n_lat, DIFF_V, tkc), lambda b, h, i: (b, h, 0, 0, 0))]
        args += [k_lat, vt_lat]
    in_specs += [pl.BlockSpec((4, DIFF_QK), lambda b, h, i: (0, 0)),
                 pl.BlockSpec((1, DIFF_V), lambda b, h, i: (0, 0))]
    args += [lam_vecs, subln.reshape(1, DIFF_V)]
    ts = _pick(tq, ATTN_SUB)
    n_sub = tq // ts
    scratch = _stream_scratch(ts, tkc, DIFF_V) * (2 * n_sub)
    return pl.pallas_call(
        functools.partial(_diff_attn_kernel, n_lat=n_lat, tkc=tkc, lam_init=lam_init, n_sub=n_sub, ts=ts),
        grid=(B, H, Tq // tq), in_specs=in_specs,
        out_specs=pl.BlockSpec((None, tq, DIFF_V), lambda b, h, i: (b, i, h)),
        out_shape=jax.ShapeDtypeStruct((B, Tq, H * DIFF_V), BF16),
        scratch_shapes=scratch, compiler_params=_params(3), name="diff_attn")(*args)


def _gelu_tanh(x):
    return 0.5 * x * (1.0 + jnp.tanh(np.sqrt(2.0 / np.pi) * (x + 0.044715 * (x * x * x))))


def _rglru_kernel(*refs, rev, tb, nt, final):
    if final:
        (x_ref, xp_ref, xn_ref, cw_ref, cb_ref, wri_ref, bri_ref, lam_ref, h0_ref, gate_ref, hf_ref,
         o_ref, ext_ref, carry_ref) = refs
    else:
        (x_ref, xp_ref, xn_ref, cw_ref, cb_ref, wri_ref, bri_ref, lam_ref, h0_ref,
         o_ref, ext_ref, carry_ref) = refs
    i = pl.program_id(1)
    blk = (nt - 1 - i) if rev else i

    @pl.when(i == 0)
    def _():
        carry_ref[...] = h0_ref[...]

    ext_ref[0:SUBLANES, :] = jnp.where(blk > 0, xp_ref[...], 0.0)
    ext_ref[SUBLANES:SUBLANES + tb, :] = x_ref[...]
    ext_ref[SUBLANES + tb:2 * SUBLANES + tb, :] = jnp.where(blk < nt - 1, xn_ref[...], 0.0)
    left = RG_CONV // 2
    xc = cb_ref[...]
    for tap in range(RG_CONV):
        off = SUBLANES - left + tap
        xc = xc + cw_ref[tap:tap + 1, :] * ext_ref[off:off + tb, :]
    xcb = xc.astype(BF16)

    rs, gs = [], []
    for kb in range(RG_BLOCKS):
        ri = _dot(xcb[:, RG_BLOCK * kb:RG_BLOCK * (kb + 1)], wri_ref[kb])
        rs.append(ri[:, :RG_BLOCK])
        gs.append(ri[:, RG_BLOCK:])
    bri = bri_ref[...]
    r = _sigmoid(jnp.concatenate(rs, axis=1) + bri[0:1])
    ig = _sigmoid(jnp.concatenate(gs, axis=1) + bri[1:2])
    lam = lam_ref[...]
    softplus_neg = jnp.maximum(-lam, 0.0) + jnp.log(1.0 + jnp.exp(-jnp.abs(lam)))
    log_a = -RG_C * r * softplus_neg
    a = jnp.exp(log_a)
    bv = jnp.sqrt(-jnp.tanh(log_a) * (a * a + 1.0)) * (ig * xc)

    row = lax.broadcasted_iota(jnp.int32, a.shape, 0)
    d = 1
    while d < tb:
        if rev:
            keep = row < tb - d
            a_sh = pltpu.roll(a, tb - d, 0)
            b_sh = pltpu.roll(bv, tb - d, 0)
        else:
            keep = row >= d
            a_sh = pltpu.roll(a, d, 0)
            b_sh = pltpu.roll(bv, d, 0)
        bv = a * jnp.where(keep, b_sh, 0.0) + bv
        a = a * jnp.where(keep, a_sh, 1.0)
        d *= 2
    h = a * carry_ref[...] + bv
    last = 0 if rev else tb - 1
    carry_ref[...] = h[last:last + 1, :]
    if final:
        o_ref[...] = ((hf_ref[...] + h) * _gelu_tanh(gate_ref[...])).astype(o_ref.dtype)
    else:
        o_ref[...] = h


def _rglru_pass(u, conv_w, conv_b, wri, bri, lam, h0, hf, rev):
    B, T, _ = u.shape
    W = RG_WIDTH
    tb = _pick(T, 256)
    nt = T // tb
    r8 = tb // SUBLANES
    final = hf is not None

    def tix(i):
        return (nt - 1 - i) if rev else i

    in_specs = [pl.BlockSpec((None, tb, U_BLK), lambda b, i: (b, tix(i), U_BX)),
                pl.BlockSpec((None, SUBLANES, U_BLK),
                             lambda b, i: (b, jnp.maximum(tix(i) * r8 - 1, 0), U_BX)),
                pl.BlockSpec((None, SUBLANES, U_BLK),
                             lambda b, i: (b, jnp.minimum((tix(i) + 1) * r8, T // SUBLANES - 1), U_BX)),
                pl.BlockSpec((RG_CONV, W), lambda b, i: (0, 0)),
                pl.BlockSpec((1, W), lambda b, i: (0, 0)),
                pl.BlockSpec((RG_BLOCKS, RG_BLOCK, 2 * RG_BLOCK), lambda b, i: (0, 0, 0)),
                pl.BlockSpec((2, W), lambda b, i: (0, 0)),
                pl.BlockSpec((1, W), lambda b, i: (0, 0)),
                pl.BlockSpec((None, 1, W), lambda b, i: (b, 0, 0))]
    args = [u, u, u, conv_w, conv_b.reshape(1, W), wri, bri, lam.reshape(1, W), h0]
    if final:
        in_specs += [pl.BlockSpec((None, tb, U_BLK), lambda b, i: (b, tix(i), U_BG)),
                     pl.BlockSpec((None, tb, W), lambda b, i: (b, tix(i), 0))]
        args += [u, hf]
    return pl.pallas_call(
        functools.partial(_rglru_kernel, rev=rev, tb=tb, nt=nt, final=final), grid=(B, nt),
        in_specs=in_specs,
        out_specs=pl.BlockSpec((None, tb, W), lambda b, i: (b, tix(i), 0)),
        out_shape=jax.ShapeDtypeStruct((B, T, W), BF16 if final else F32),
        scratch_shapes=[pltpu.VMEM((tb + 2 * SUBLANES, W), F32), pltpu.VMEM((1, W), F32)],
        compiler_params=_params(2), name="rglru_bwd" if rev else "rglru_fwd")(*args)


HGRN_CHUNK = 128
HGRN_SUB = 8


def _hgrn_chunk(qv, z, v, lb, st, ones, rev):
    C = HGRN_CHUNK
    sg = _sigmoid(z)
    k = (1.0 - lb) * _sigmoid(-z)
    g = jnp.log(lb + (1.0 - lb) * sg)
    row = lax.broadcasted_iota(jnp.int32, (C, LANES), 0)
    col = lax.broadcasted_iota(jnp.int32, (C, C), 1)
    rowc = lax.broadcasted_iota(jnp.int32, (C, C), 0)

    G = g
    d = 1
    while d < C:
        if rev:
            G = G + jnp.where(row < C - d, pltpu.roll(G, C - d, 0), 0.0)
        else:
            G = G + jnp.where(row >= d, pltpu.roll(G, d, 0), 0.0)
        d *= 2
    last = 0 if rev else C - 1
    g_last = G[last:last + 1, :]

    qe = (qv * jnp.exp(G)).astype(BF16)
    o = _dot_nt(qe, st.astype(BF16))
    kdec = (k * jnp.exp(g_last - G)).astype(BF16)
    vb = v.astype(BF16)
    st_new = st * jnp.exp(g_last) + _dot(v.T.astype(BF16), kdec)

    pos = (C - 1 - row) if rev else row
    a_sum = jnp.zeros((C, C), F32)
    hsz = HGRN_SUB
    while hsz < C:
        grp = 2 * hsz
        ng = C // grp
        G3 = G.reshape(ng, grp, LANES)
        ref = G3[:, hsz:hsz + 1, :] if rev else G3[:, hsz - 1:hsz, :]
        ref = jnp.broadcast_to(ref, (ng, grp, LANES)).reshape(C, LANES)
        e = jnp.exp(-jnp.abs(G - ref))
        later = ((pos >> int(np.log2(hsz))) & 1) == 1
        ql = jnp.where(later, qv * e, 0.0).astype(BF16)
        kl = jnp.where(later, 0.0, k * e).astype(BF16)
        al = _dot_nt(ql, kl)
        shift = int(np.log2(grp))
        a_sum = a_sum + jnp.where((rowc >> shift) == (col >> shift), al, 0.0)
        hsz = grp
    o = o + _dot(a_sum.astype(BF16), vb)

    nb = C // HGRN_SUB
    q3 = qv.reshape(nb, HGRN_SUB, LANES)
    k3 = k.reshape(nb, HGRN_SUB, LANES)
    v3 = v.reshape(nb, HGRN_SUB, LANES)
    G3 = G.reshape(nb, HGRN_SUB, LANES)
    rb = lax.broadcasted_iota(jnp.int32, (nb, HGRN_SUB, LANES), 1)
    od = jnp.zeros((nb, HGRN_SUB, LANES), F32)
    for s in range(HGRN_SUB):
        valid = (rb <= s) if rev else (rb >= s)
        e = jnp.exp(jnp.where(valid, G3 - G3[:, s:s + 1, :], 0.0))
        p = jnp.where(valid, q3 * k3[:, s:s + 1, :] * e, 0.0)
        rs = _dot(p.reshape(C, LANES).astype(BF16), ones)
        od = od + rs.reshape(nb, HGRN_SUB, LANES) * v3[:, s:s + 1, :]
    o = o + od.reshape(C, LANES)
    return o, st_new


def _hgrn_kernel(*refs, rev, tb, nt, final):
    if final:
        (q_ref, z_ref, v_ref, lb_ref, s0_ref, ones_ref, of_ref, og_ref, gn_ref,
         o_ref, sfin_ref, st_ref) = refs
    else:
        q_ref, z_ref, v_ref, lb_ref, s0_ref, ones_ref, o_ref, sfin_ref, st_ref = refs
    i = pl.program_id(2)

    @pl.when(i == 0)
    def _():
        st_ref[...] = s0_ref[...]

    lb = lb_ref[...]
    ones = ones_ref[...]
    C = HGRN_CHUNK
    nch = tb // C

    def body(c, carry):
        cc = (nch - 1 - c) if rev else c
        r0 = pl.multiple_of(cc * C, C)
        qin = q_ref[pl.ds(r0, C), :]
        qv = qin * _sigmoid(qin)
        o, st_new = _hgrn_chunk(qv, z_ref[pl.ds(r0, C), :], v_ref[pl.ds(r0, C), :], lb,
                                st_ref[...], ones, rev)
        st_ref[...] = st_new
        if final:
            o = o + of_ref[pl.ds(r0, C), :]
            og = og_ref[pl.ds(r0, C), :]
            o = _rms(o, gn_ref[...]) * (og * _sigmoid(og))
        o_ref[pl.ds(r0, C), :] = o.astype(o_ref.dtype)
        return carry

    lax.fori_loop(0, nch, body, 0)

    @pl.when(i == nt - 1)
    def _():
        sfin_ref[...] = st_ref[...]


def _hgrn_pass(u, z_blk, lb, s0, ones, of, g_norm, rev):
    B, T, _ = u.shape
    H = HGRN_HEADS
    tb = _pick(T, 512)
    nt = T // tb
    final = of is not None

    def tix(i):
        return (nt - 1 - i) if rev else i

    def ublk(blk):
        return pl.BlockSpec((None, tb, LANES), lambda b, h, i: (b, tix(i), blk * (U_BLK // LANES) + h))

    in_specs = [ublk(U_DQ), ublk(z_blk), ublk(U_DI),
                pl.BlockSpec((1, LANES), lambda b, h, i: (0, h)),
                pl.BlockSpec((None, None, HGRN_V, HGRN_K), lambda b, h, i: (b, h, 0, 0)),
                pl.BlockSpec((LANES, LANES), lambda b, h, i: (0, 0))]
    args = [u, u, u, lb.reshape(1, -1), s0, ones]
    if final:
        in_specs += [pl.BlockSpec((None, tb, LANES), lambda b, h, i: (b, tix(i), h)),
                     ublk(U_DG),
                     pl.BlockSpec((1, HGRN_V), lambda b, h, i: (0, 0))]
        args += [of, u, g_norm.reshape(1, HGRN_V)]
    return pl.pallas_call(
        functools.partial(_hgrn_kernel, rev=rev, tb=tb, nt=nt, final=final), grid=(B, H, nt),
        in_specs=in_specs,
        out_specs=[pl.BlockSpec((None, tb, LANES), lambda b, h, i: (b, tix(i), h)),
                   pl.BlockSpec((None, None, HGRN_V, HGRN_K), lambda b, h, i: (b, h, 0, 0))],
        out_shape=[jax.ShapeDtypeStruct((B, T, H * HGRN_V), BF16 if final else F32),
                   jax.ShapeDtypeStruct((B, H, HGRN_V, HGRN_K), F32)],
        scratch_shapes=[pltpu.VMEM((HGRN_V, HGRN_K), F32)],
        compiler_params=_params(3), name="hgrn_bwd" if rev else "hgrn_fwd")(*args)


def _merge_kernel(x_ref, mod_ref, pre_ref, post_ref, ya_ref, yb_ref, yc_ref, yd_ref, w_ref, wb_ref, mb_ref,
                  o_ref, h_ref, acc_ref):
    n = pl.program_id(2)

    @pl.when(n == 0)
    def _():
        h_ref[...] = _prenorm(x_ref[...], mod_ref, 1, pre_ref[...]).astype(BF16)
        acc_ref[...] = jnp.zeros(acc_ref.shape, F32)

    for nb, y_ref in enumerate((ya_ref, yb_ref, yc_ref, yd_ref)):
        @pl.when(n == nb)
        def _(y_ref=y_ref):
            gate = _sigmoid(_dot(h_ref[...], w_ref[...]) + mb_ref[...])
            acc_ref[...] += gate * _dot(y_ref[...], wb_ref[...])

    @pl.when(n == N_BRANCH)
    def _():
        mix = _dot(acc_ref[...].astype(BF16), w_ref[...])
        o_ref[...] = x_ref[...] + mod_ref[5:6, :] * _rms(mix, post_ref[...])


def _merge(x, mod, pre_g, post_g, ys, w5, wb, mb):
    B, T, D = x.shape
    tm = _pick(T, 512)
    last = N_BRANCH - 1
    yspec = pl.BlockSpec((None, tm, BRANCH_W), lambda b, t, n: (b, t, 0))
    return pl.pallas_call(
        _merge_kernel, grid=(B, T // tm, N_BRANCH + 1),
        in_specs=[pl.BlockSpec((None, tm, D), lambda b, t, n: (b, t, 0)),
                  pl.BlockSpec((None, N_MOD, D), lambda b, t, n: (b, 0, 0)),
                  pl.BlockSpec((1, D), lambda b, t, n: (0, 0)),
                  pl.BlockSpec((1, D), lambda b, t, n: (0, 0)),
                  yspec, yspec, yspec, yspec,
                  pl.BlockSpec((None, D, D), lambda b, t, n: (n, 0, 0)),
                  pl.BlockSpec((None, BRANCH_W, D), lambda b, t, n: (jnp.minimum(n, last), 0, 0)),
                  pl.BlockSpec((None, 1, D), lambda b, t, n: (jnp.minimum(n, last), 0, 0))],
        out_specs=pl.BlockSpec((None, tm, D), lambda b, t, n: (b, t, 0)),
        out_shape=jax.ShapeDtypeStruct(x.shape, F32),
        scratch_shapes=[pltpu.VMEM((tm, D), BF16), pltpu.VMEM((tm, D), F32)],
        compiler_params=_params(3), name="merge")(
            x, mod, pre_g.reshape(1, D), post_g.reshape(1, D), *ys, w5, wb, mb)


def _rope_tables(n_tok):
    n_rows = n_tok // GRID_W
    row = jnp.repeat(jnp.arange(n_rows, dtype=F32), GRID_W)
    col = jnp.tile(jnp.arange(GRID_W, dtype=F32), n_rows)
    axis_dim = MLA_ROPE // 2
    inv_freq = ROPE_BASE ** (-jnp.arange(0, axis_dim, 2, dtype=F32) / axis_dim)
    ar = row[:, None] * inv_freq
    ac = col[:, None] * inv_freq
    cos64 = jnp.concatenate([jnp.cos(ar), jnp.cos(ar), jnp.cos(ac), jnp.cos(ac)], axis=-1)
    sin64 = jnp.concatenate([-jnp.sin(ar), jnp.sin(ar), -jnp.sin(ac), jnp.sin(ac)], axis=-1)
    return cos64, sin64


def _layer_weights(l, w_in, mla_w_uq, mla_w_ukv, rg_w_r, rg_w_i, rg_b_r, rg_b_i, w_out):
    D = D_MODEL
    cuts = np.cumsum([MLA_Q_LORA, MLA_KV_LORA, MLA_ROPE, RG_WIDTH, RG_WIDTH, 512, 512, 512,
                      512, 512, 512, 512, 512])
    wi = w_in[l]
    parts = jnp.split(wi[:, :cuts[-1]], cuts[:-1].tolist(), axis=1)
    (a_q, a_kv, a_kr, b_x, b_g, c_q, c_k, c_v, d_q, d_ff, d_fb, d_i, d_g) = parts
    pad = jnp.zeros((D, U_BLK - MLA_KV_LORA - MLA_ROPE), wi.dtype)
    w_u = jnp.concatenate([a_q, b_x, b_g, c_q, c_k, c_v, d_q, d_ff, d_fb, d_i, d_g, a_kv, a_kr, pad],
                          axis=1).astype(BF16)
    w_gates = wi[:, cuts[-1]:].reshape(D, N_BRANCH, D).transpose(1, 0, 2)
    w5 = jnp.concatenate([w_gates, w_out[l][None]], axis=0).astype(BF16)

    wq = mla_w_uq[l].reshape(MLA_Q_LORA, MLA_HEADS, MLA_NOPE + MLA_ROPE)
    wq = jnp.pad(wq, ((0, 0), (0, 0), (0, 256 - MLA_NOPE - MLA_ROPE))).reshape(MLA_Q_LORA, -1).astype(BF16)
    wkv = mla_w_ukv[l].reshape(MLA_KV_LORA, MLA_HEADS, MLA_NOPE + MLA_V)
    wk = wkv[:, :, :MLA_NOPE].reshape(MLA_KV_LORA, -1).astype(BF16)
    wvt = wkv[:, :, MLA_NOPE:].reshape(MLA_KV_LORA, -1).T.astype(BF16)

    wri = [jnp.concatenate([rg_w_r[l, d], rg_w_i[l, d]], axis=-1).astype(BF16) for d in range(2)]
    bri = [jnp.stack([rg_b_r[l, d], rg_b_i[l, d]], axis=0) for d in range(2)]
    return w_u, w5, wq, wk, wvt, wri, bri


def kernel(x, c, ctx, c_ctx, ada_w, ada_b, pre_norm, post_norm, ffn_w_gate, ffn_w_up, ffn_w_down, w_in, mla_q_norm, mla_w_uq, mla_kv_norm, mla_w_ukv, rg_conv_w, rg_conv_b, rg_w_r, rg_b_r, rg_w_i, rg_b_i, rg_lambda, diff_lambda, diff_subln, hgrn_lb_logits, hgrn_norm, merge_b, w_branch, w_out):
    B, S, D = x.shape
    Tc = ctx.shape[1]
    depth = ada_w.shape[0]
    tkc = _pick(S, 512)

    cos64, sin64 = _rope_tables(S)
    one64 = jnp.ones((S, 64), F32)
    zero64 = jnp.zeros((S, 64), F32)
    rc_mla = jnp.concatenate([cos64, one64], axis=1)
    rs_mla = jnp.concatenate([sin64, zero64], axis=1)
    rc_diff = jnp.concatenate([cos64, cos64], axis=1)
    rs_diff = jnp.concatenate([sin64, sin64], axis=1)
    rc_ctx = jnp.ones((Tc, LANES), F32)
    rs_ctx = jnp.zeros((Tc, LANES), F32)

    lb_soft = jax.nn.softmax(hgrn_lb_logits.astype(F32), axis=0)
    hgrn_lb = jnp.cumsum(lb_soft, axis=0) - lb_soft[0]

    c8 = jnp.zeros((SUBLANES, D), F32).at[:B].set(c).at[B].set(c_ctx)
    eye = jnp.eye(DIFF_HEADS * DIFF_V, dtype=BF16)
    ones = jnp.ones((LANES, LANES), BF16)
    zero_state = jnp.zeros((B, HGRN_HEADS, HGRN_V, HGRN_K), F32)
    zero_h = jnp.zeros((B, 1, RG_WIDTH), F32)

    xc = ctx
    for l in range(depth):
        need_ctx = l < depth - 1
        lam_init = 0.8 - 0.6 * float(np.exp(-0.3 * l))
        mod8 = _ada_mod(c8, ada_w[l], ada_b[l]).reshape(SUBLANES, N_MOD, D)
        mod = mod8[:B]
        mod_c = jnp.broadcast_to(mod8[B:B + 1], (B, N_MOD, D))
        w_u, w5, wq, wk, wvt, wri, bri = _layer_weights(
            l, w_in, mla_w_uq, mla_w_ukv, rg_w_r, rg_w_i, rg_b_r, rg_b_i, w_out)
        wg = [ffn_w_gate[l, i].astype(BF16) for i in range(2)]
        wu = [ffn_w_up[l, i].astype(BF16) for i in range(2)]
        wd = [ffn_w_down[l, i].astype(BF16) for i in range(2)]
        wb = w_branch[l].astype(BF16)
        mb = merge_b[l].reshape(N_BRANCH, 1, D)

        x = _ffn(x, mod, 0, pre_norm[l, 0], post_norm[l, 0], wg[0], wu[0], wd[0])
        xc = _ffn(xc, mod_c, 0, pre_norm[l, 0], post_norm[l, 0], wg[0], wu[0], wd[0])

        u = _inproj(x, mod, pre_norm[l, 1], w_u)
        uc = _inproj(xc, mod_c, pre_norm[l, 1], w_u)

        qa_c, ka_c, vta_c = _mla_prep(uc, rc_ctx, rs_ctx, mla_q_norm[l], mla_kv_norm[l], wq, wk, wvt, Tc, False)
        qa, ka, vta = _mla_prep(u, rc_mla, rs_mla, mla_q_norm[l], mla_kv_norm[l], wq, wk, wvt, tkc, True)
        y_a = _mla_attn(qa, ka_c, vta_c, ka, vta)

        qd_c, kd_c, vtd_c = _diff_prep(uc, rc_ctx, rs_ctx, eye, Tc, False)
        qd, kd, vtd = _diff_prep(u, rc_diff, rs_diff, eye, tkc, True)
        y_c = _diff_attn(qd, kd_c, vtd_c, kd, vtd, diff_lambda[l], diff_subln[l], lam_init)

        hc_f = _rglru_pass(uc, rg_conv_w[l], rg_conv_b[l], wri[0], bri[0], rg_lambda[l, 0], zero_h, None, False)
        hl_f = _rglru_pass(u, rg_conv_w[l], rg_conv_b[l], wri[0], bri[0], rg_lambda[l, 0],
                           hc_f[:, Tc - 1:Tc, :], None, False)
        if need_ctx:
            y_b_c = _rglru_pass(uc, rg_conv_w[l], rg_conv_b[l], wri[1], bri[1], rg_lambda[l, 1], zero_h, hc_f, True)
        hc_b = _rglru_pass(uc, rg_conv_w[l], rg_conv_b[l], wri[1], bri[1], rg_lambda[l, 1], zero_h, None, True)
        y_b = _rglru_pass(u, rg_conv_w[l], rg_conv_b[l], wri[1], bri[1], rg_lambda[l, 1],
                          hc_b[:, 0:1, :], hl_f, True)

        oc_f, s_f = _hgrn_pass(uc, U_DFF, hgrn_lb[l], zero_state, ones, None, None, False)
        if need_ctx:
            y_d_c, s_b = _hgrn_pass(uc, U_DFB, hgrn_lb[l], zero_state, ones, oc_f, hgrn_norm[l], True)
        else:
            _, s_b = _hgrn_pass(uc, U_DFB, hgrn_lb[l], zero_state, ones, None, None, True)
        o_f, _ = _hgrn_pass(u, U_DFF, hgrn_lb[l], s_f, ones, None, None, False)
        y_d, _ = _hgrn_pass(u, U_DFB, hgrn_lb[l], s_b, ones, o_f, hgrn_norm[l], True)

        x = _merge(x, mod, pre_norm[l, 1], post_norm[l, 1], (y_a, y_b, y_c, y_d), w5, wb, mb)
        if need_ctx:
            y_a_c = _mla_attn(qa_c, ka_c, vta_c, None, None)
            y_c_c = _diff_attn(qd_c, kd_c, vtd_c, None, None, diff_lambda[l], diff_subln[l], lam_init)
            xc = _merge(xc, mod_c, pre_norm[l, 1], post_norm[l, 1], (y_a_c, y_b_c, y_c_c, y_d_c), w5, wb, mb)
            xc = _ffn(xc, mod_c, 2, pre_norm[l, 2], post_norm[l, 2], wg[1], wu[1], wd[1])

        x = _ffn(x, mod, 2, pre_norm[l, 2], post_norm[l, 2], wg[1], wu[1], wd[1])
    return x
```

```python
import functools
from typing import Any, NamedTuple

import jax
import jax.numpy as jnp
import numpy as np
from jax import lax
from jax.experimental import pallas as pl
from jax.experimental.pallas import tpu as pltpu

F32 = jnp.float32
BF16 = jnp.bfloat16

D_MODEL = 2048
GRID_W = 64
N_SUB = 3
N_MOD = 3 * N_SUB
FFN_RES = 0.5
NORM_EPS = 1e-6
ROPE_BASE = 10000.0

MLA_HEADS = 4
MLA_Q_LORA = 512
MLA_KV_LORA = 256
MLA_NOPE = 128
MLA_ROPE = 64
MLA_V = 128
MLA_SCALE = (MLA_NOPE + MLA_ROPE) ** -0.5

RG_WIDTH = 512
RG_BLOCKS = 4
RG_BLOCK = RG_WIDTH // RG_BLOCKS
RG_CONV = 4
RG_C = 8.0

DIFF_HEADS = 4
DIFF_QK = 64
DIFF_V = 2 * DIFF_QK
DIFF_SCALE = DIFF_QK ** -0.5

HGRN_HEADS = 4
HGRN_K = 128
HGRN_V = 128

N_BRANCH = 4
BRANCH_W = 512

LANES = 128
SUBLANES = 8
VMEM_LIMIT_BYTES = 56 * 1024 * 1024

NT_DIMS = (((1,), (1,)), ((), ()))


def _params(n_axes):
    return pltpu.CompilerParams(dimension_semantics=("arbitrary",) * n_axes,
                                vmem_limit_bytes=VMEM_LIMIT_BYTES)


def _pick(n, pref):
    t = min(n, pref)
    assert n % t == 0, (n, pref)
    return t


def _rms(x, g):
    return x * lax.rsqrt(jnp.mean(x * x, axis=-1, keepdims=True) + NORM_EPS) * g


def _sigmoid(x):
    return jax.nn.sigmoid(x)


def _dot(a, b):
    return jnp.dot(a, b, preferred_element_type=F32)


def _dot_nt(a, b):
    return lax.dot_general(a, b, NT_DIMS, preferred_element_type=F32)


def _ada_kernel(c_ref, w_ref, b_ref, o_ref):
    c = c_ref[...]
    cond = (c * _sigmoid(c)).astype(BF16)
    o_ref[...] = _dot(cond, w_ref[...].astype(BF16)) + b_ref[...]


def _ada_mod(c8, w, b, l):
    L, D, N = w.shape
    tn = _pick(N, 1024)
    return pl.pallas_call(
        _ada_kernel, grid=(N // tn,),
        in_specs=[pl.BlockSpec((SUBLANES, D), lambda n: (0, 0)),
                  pl.BlockSpec((None, D, tn), lambda n: (l, 0, n)),
                  pl.BlockSpec((None, 1, tn), lambda n: (l, 0, n))],
        out_specs=pl.BlockSpec((SUBLANES, tn), lambda n: (0, n)),
        out_shape=jax.ShapeDtypeStruct((SUBLANES, N), F32),
        compiler_params=_params(1), name="ada_mod")(c8, w, b.reshape(L, 1, N))


def _prenorm(x, mod_ref, j, g):
    return _rms(x, g) * (1.0 + mod_ref[3 * j + 1:3 * j + 2, :]) + mod_ref[3 * j:3 * j + 1, :]


def _ffn_kernel(x_ref, mod_ref, pre_ref, post_ref, wg_ref, wu_ref, wd_ref, o_ref, h_ref, acc_ref,
                *, j, nf):
    f = pl.program_id(2)

    @pl.when(f == 0)
    def _():
        h_ref[...] = _prenorm(x_ref[...], mod_ref, j, pre_ref[...]).astype(BF16)
        acc_ref[...] = jnp.zeros(acc_ref.shape, F32)

    h = h_ref[...]
    g = _dot(h, wg_ref[...])
    u = _dot(h, wu_ref[...])
    a = (g * _sigmoid(g) * u).astype(BF16)
    acc_ref[...] += _dot(a, wd_ref[...])

    @pl.when(f == nf - 1)
    def _():
        yn = _rms(acc_ref[...], post_ref[...])
        o_ref[...] = x_ref[...] + FFN_RES * mod_ref[3 * j + 2:3 * j + 3, :] * yn


def _ffn(x, mod, j, pre_g, post_g, wg, wu, wd, l, i):
    B, T, D = x.shape
    F = wg.shape[-1]
    tm = _pick(T, 512)
    tf = _pick(F, 512)
    nf = F // tf
    return pl.pallas_call(
        functools.partial(_ffn_kernel, j=j, nf=nf), grid=(B, T // tm, nf),
        in_specs=[pl.BlockSpec((None, tm, D), lambda b, t, f: (b, t, 0)),
                  pl.BlockSpec((None, N_MOD, D), lambda b, t, f: (b, 0, 0)),
                  pl.BlockSpec((1, D), lambda b, t, f: (0, 0)),
                  pl.BlockSpec((1, D), lambda b, t, f: (0, 0)),
                  pl.BlockSpec((None, None, D, tf), lambda b, t, f: (l, i, 0, f)),
                  pl.BlockSpec((None, None, D, tf), lambda b, t, f: (l, i, 0, f)),
                  pl.BlockSpec((None, None, tf, D), lambda b, t, f: (l, i, f, 0))],
        out_specs=pl.BlockSpec((None, tm, D), lambda b, t, f: (b, t, 0)),
        out_shape=jax.ShapeDtypeStruct(x.shape, F32),
        scratch_shapes=[pltpu.VMEM((tm, D), BF16), pltpu.VMEM((tm, D), F32)],
        compiler_params=_params(3), name="ffn")(
            x, mod, pre_g.reshape(1, D), post_g.reshape(1, D), wg, wu, wd)


def _inproj_kernel(x_ref, mod_ref, pre_ref, w_ref, o_ref, h_ref):
    @pl.when(pl.program_id(2) == 0)
    def _():
        h_ref[...] = _prenorm(x_ref[...], mod_ref, 1, pre_ref[...]).astype(BF16)

    o_ref[...] = _dot(h_ref[...], w_ref[...])


def _inproj(x, mod, pre_g, w):
    B, T, D = x.shape
    N = w.shape[1]
    tm = _pick(T, 1024)
    tn = _pick(N, 512)
    return pl.pallas_call(
        _inproj_kernel, grid=(B, T // tm, N // tn),
        in_specs=[pl.BlockSpec((None, tm, D), lambda b, t, n: (b, t, 0)),
                  pl.BlockSpec((None, N_MOD, D), lambda b, t, n: (b, 0, 0)),
                  pl.BlockSpec((1, D), lambda b, t, n: (0, 0)),
                  pl.BlockSpec((D, tn), lambda b, t, n: (0, n))],
        out_specs=pl.BlockSpec((None, tm, tn), lambda b, t, n: (b, t, n)),
        out_shape=jax.ShapeDtypeStruct((B, T, N), F32),
        scratch_shapes=[pltpu.VMEM((tm, D), BF16)],
        compiler_params=_params(3), name="inproj")(x, mod, pre_g.reshape(1, D), w)


U_AQ, U_BX, U_BG, U_CQ, U_CK, U_CV, U_DQ, U_DFF, U_DFB, U_DI, U_DG, U_AKV = range(12)
U_BLK = 512


def _rope128(x, c, s):
    lane = lax.broadcasted_iota(jnp.int32, x.shape, 1)
    partner = jnp.where((lane & 31) < 16, pltpu.roll(x, LANES - 16, 1), pltpu.roll(x, 16, 1))
    return x * c + partner * s


def _mla_prep_kernel(uq_ref, ukv_ref, c_ref, s_ref, qn_ref, kvn_ref, wq_ref, wk_ref, wvt_ref,
                     q_ref, k_ref, vt_ref, *, rope):
    uqn = _rms(uq_ref[...], qn_ref[...]).astype(BF16)
    q = _dot(uqn, wq_ref[...]) * (MLA_SCALE * LOG2E)
    ukv = ukv_ref[...]
    ukvn = _rms(ukv[:, :MLA_KV_LORA], kvn_ref[...]).astype(BF16)
    kn = _dot(ukvn, wk_ref[...])
    kr = ukv[:, MLA_KV_LORA:MLA_KV_LORA + LANES]
    if rope:
        c = c_ref[...]
        s = s_ref[...]
        kr = _rope128(kr, c, s)
    krb = kr.astype(BF16)
    for h in range(MLA_HEADS):
        qr = q[:, 256 * h + 128:256 * h + 256]
        if rope:
            qr = _rope128(qr, c, s)
        q_ref[:, 256 * h:256 * h + 128] = q[:, 256 * h:256 * h + 128].astype(BF16)
        q_ref[:, 256 * h + 128:256 * h + 256] = qr.astype(BF16)
        k_ref[:, 256 * h:256 * h + 128] = kn[:, 128 * h:128 * h + 128].astype(BF16)
        k_ref[:, 256 * h + 128:256 * h + 256] = krb
    vt = _dot_nt(wvt_ref[...], ukvn)
    tm = vt.shape[1]
    vt_ref[...] = vt.reshape(MLA_HEADS, MLA_V, tm).astype(BF16)


def _mla_prep(u, rc, rs, q_norm, kv_norm, wq, wk, wvt, tkc, rope):
    B, T, _ = u.shape
    tm = tkc
    nC = T // tm
    H = MLA_HEADS
    return pl.pallas_call(
        functools.partial(_mla_prep_kernel, rope=rope), grid=(B, nC),
        in_specs=[pl.BlockSpec((None, tm, U_BLK), lambda b, t: (b, t, U_AQ)),
                  pl.BlockSpec((None, tm, U_BLK), lambda b, t: (b, t, U_AKV)),
                  pl.BlockSpec((tm, LANES), lambda b, t: (t, 0)),
                  pl.BlockSpec((tm, LANES), lambda b, t: (t, 0)),
                  pl.BlockSpec((1, MLA_Q_LORA), lambda b, t: (0, 0)),
                  pl.BlockSpec((1, MLA_KV_LORA), lambda b, t: (0, 0)),
                  pl.BlockSpec(wq.shape, lambda b, t: (0, 0)),
                  pl.BlockSpec(wk.shape, lambda b, t: (0, 0)),
                  pl.BlockSpec(wvt.shape, lambda b, t: (0, 0))],
        out_specs=[pl.BlockSpec((None, tm, H * 256), lambda b, t: (b, t, 0)),
                   pl.BlockSpec((None, tm, H * 256), lambda b, t: (b, t, 0)),
                   pl.BlockSpec((None, H, None, MLA_V, tm), lambda b, t: (b, 0, t, 0, 0))],
        out_shape=[jax.ShapeDtypeStruct((B, T, H * 256), BF16),
                   jax.ShapeDtypeStruct((B, T, H * 256), BF16),
                   jax.ShapeDtypeStruct((B, H, nC, MLA_V, tm), BF16)],
        compiler_params=_params(2), name="mla_prep")(
            u, u, rc, rs, q_norm.reshape(1, -1), kv_norm.reshape(1, -1), wq, wk, wvt)


ATTN_SUB = 512
MLA_TQ = 1024
DIFF_TQ = 512
MLA_TKC = 1024
DIFF_TKC = 1024
LOG2E = float(np.log2(np.e))


class _Stream(NamedTuple):
    m: Any
    l: Any
    acc: Any
    s: Any = None
    cm: Any = None
    p: Any = None
    al: Any = None


def _stream_scratch(tq, tkc, dv):
    shapes = [pltpu.VMEM((1, tq), F32), pltpu.VMEM((1, tq), F32), pltpu.VMEM((dv, tq), F32)]
    if tkc:
        shapes += [pltpu.VMEM((2, tkc, tq), F32), pltpu.VMEM((2, 1, tq), F32),
                   pltpu.VMEM((2, tkc, tq), BF16), pltpu.VMEM((2, 1, tq), F32)]
    return shapes


def _load_block(ref, r0, nr, c0, nc):
    return ref[r0:r0 + nr, c0:c0 + nc]


def _softmax_step(k, vt, q, st):
    s = _dot_nt(k(), q())
    m_prev = st.m[...]
    m_new = jnp.maximum(m_prev, jnp.max(s, axis=0, keepdims=True))
    alpha = jnp.exp2(m_prev - m_new)
    p = jnp.exp2(s - m_new)
    st.l[...] = alpha * st.l[...] + jnp.sum(p, axis=0, keepdims=True)
    st.acc[...] = alpha * st.acc[...] + _dot(vt(), p.astype(BF16))
    st.m[...] = m_new


def _stage_scores(k, qs, streams, slot):
    for q, st in zip(qs, streams):
        s = _dot_nt(k(), q())
        st.s[slot] = s
        st.cm[slot] = jnp.max(s, axis=0, keepdims=True)


def _stage_softmax(streams, slot):
    for st in streams:
        m_prev = st.m[...]
        m_new = jnp.maximum(m_prev, st.cm[slot])
        alpha = jnp.exp2(m_prev - m_new)
        p = jnp.exp2(st.s[slot] - m_new)
        st.l[...] = alpha * st.l[...] + jnp.sum(p, axis=0, keepdims=True)
        st.p[slot] = p.astype(BF16)
        st.al[slot] = alpha
        st.m[...] = m_new


def _stage_values(vt, streams, slot):
    for st in streams:
        st.acc[...] = st.al[slot] * st.acc[...] + _dot(vt(), st.p[slot])


def _attend(qs, kc_ref, vtc_ref, kl_ref, vtl_ref, n_lat, tkc, streams):
    for st in streams:
        st.m[...] = jnp.full(st.m.shape, -jnp.inf, F32)
        st.l[...] = jnp.zeros(st.l.shape, F32)
        st.acc[...] = jnp.zeros(st.acc.shape, F32)
    for q, st in zip(qs, streams):
        _softmax_step(lambda: kc_ref[...], lambda: vtc_ref[...], q, st)
    if not n_lat:
        return

    def keys(j):
        return lambda: kl_ref[pl.ds(pl.multiple_of(j * tkc, tkc), tkc), :]

    def vals(j):
        return lambda: vtl_ref[j]

    if n_lat < 4 or n_lat % 2:
        def plain(j, carry):
            for q, st in zip(qs, streams):
                _softmax_step(keys(j), vals(j), q, st)
            return carry
        lax.fori_loop(0, n_lat, plain, 0)
        return

    _stage_scores(keys(0), qs, streams, 0)
    _stage_scores(keys(1), qs, streams, 1)
    _stage_softmax(streams, 0)

    def body(i, carry):
        j = 2 * i
        _stage_scores(keys(j), qs, streams, 0)
        _stage_softmax(streams, 1)
        _stage_values(vals(j - 2), streams, 0)
        _stage_scores(keys(j + 1), qs, streams, 1)
        _stage_softmax(streams, 0)
        _stage_values(vals(j - 1), streams, 1)
        return carry

    lax.fori_loop(1, n_lat // 2, body, 0)
    _stage_softmax(streams, 1)
    _stage_values(vals(n_lat - 2), streams, 0)
    _stage_values(vals(n_lat - 1), streams, 1)


def _split_streams(scratch, n):
    per = len(scratch) // n
    return [_Stream(*scratch[i * per:(i + 1) * per]) for i in range(n)]


def _mla_attn_kernel(*refs, n_lat, tkc, n_sub, ts):
    if n_lat:
        q_ref, kc_ref, vtc_ref, kl_ref, vtl_ref, o_ref = refs[:6]
        scratch = refs[6:]
    else:
        q_ref, kc_ref, vtc_ref, o_ref = refs[:4]
        kl_ref = vtl_ref = None
        scratch = refs[4:]
    streams = _split_streams(scratch, n_sub)
    qs = [functools.partial(_load_block, q_ref, sub * ts, ts, 0, q_ref.shape[1]) for sub in range(n_sub)]
    _attend(qs, kc_ref, vtc_ref, kl_ref, vtl_ref, n_lat, tkc, streams)
    for sub, st in enumerate(streams):
        o = st.acc[...] / st.l[...]
        o_ref[sub * ts:(sub + 1) * ts, :] = o.T.astype(o_ref.dtype)


def _mla_attn(q, k_ctx, vt_ctx, k_lat, vt_lat):
    B, Tq, _ = q.shape
    H = MLA_HEADS
    Tc = k_ctx.shape[1]
    tq = _pick(Tq, MLA_TQ)
    n_lat = 0 if k_lat is None else vt_lat.shape[2]
    tkc = 0 if k_lat is None else vt_lat.shape[4]
    in_specs = [pl.BlockSpec((None, tq, 256), lambda b, h, i: (b, i, h)),
                pl.BlockSpec((None, Tc, 256), lambda b, h, i: (b, 0, h)),
                pl.BlockSpec((None, None, None, MLA_V, Tc), lambda b, h, i: (b, h, 0, 0, 0))]
    args = [q, k_ctx, vt_ctx]
    if n_lat:
        T = k_lat.shape[1]
        in_specs += [pl.BlockSpec((None, T, 256), lambda b, h, i: (b, 0, h)),
                     pl.BlockSpec((None, None, n_lat, MLA_V, tkc), lambda b, h, i: (b, h, 0, 0, 0))]
        args += [k_lat, vt_lat]
    ts = _pick(tq, ATTN_SUB)
    n_sub = tq // ts
    return pl.pallas_call(
        functools.partial(_mla_attn_kernel, n_lat=n_lat, tkc=tkc, n_sub=n_sub, ts=ts),
        grid=(B, H, Tq // tq), in_specs=in_specs,
        out_specs=pl.BlockSpec((None, tq, MLA_V), lambda b, h, i: (b, i, h)),
        out_shape=jax.ShapeDtypeStruct((B, Tq, H * MLA_V), BF16),
        scratch_shapes=_stream_scratch(ts, tkc, MLA_V) * n_sub,
        compiler_params=_params(3), name="mla_attn")(*args)


def _diff_prep_kernel(uq_ref, uk_ref, uv_ref, c_ref, s_ref, eye_ref, q_ref, k_ref, vt_ref, *, rope):
    uq = uq_ref[...] * (DIFF_SCALE * LOG2E)
    uk = uk_ref[...]
    lane = lax.broadcasted_iota(jnp.int32, (uq.shape[0], LANES), 1)
    if rope:
        c = c_ref[...]
        s = s_ref[...]
    for h in range(DIFF_HEADS):
        qh = uq[:, LANES * h:LANES * (h + 1)]
        kh = uk[:, LANES * h:LANES * (h + 1)]
        if rope:
            qh = _rope128(qh, c, s)
            kh = _rope128(kh, c, s)
        q_ref[:, 256 * h:256 * h + 128] = jnp.where(lane < DIFF_QK, qh, 0.0).astype(BF16)
        q_ref[:, 256 * h + 128:256 * h + 256] = jnp.where(lane < DIFF_QK, 0.0, qh).astype(BF16)
        k_ref[:, LANES * h:LANES * (h + 1)] = kh.astype(BF16)
    vt = _dot_nt(eye_ref[...], uv_ref[...].astype(BF16))
    tm = vt.shape[1]
    vt_ref[...] = vt.reshape(DIFF_HEADS, DIFF_V, tm).astype(BF16)


def _diff_prep(u, rc, rs, eye, tkc, rope):
    B, T, _ = u.shape
    tm = tkc
    nC = T // tm
    H = DIFF_HEADS
    return pl.pallas_call(
        functools.partial(_diff_prep_kernel, rope=rope), grid=(B, nC),
        in_specs=[pl.BlockSpec((None, tm, U_BLK), lambda b, t: (b, t, U_CQ)),
                  pl.BlockSpec((None, tm, U_BLK), lambda b, t: (b, t, U_CK)),
                  pl.BlockSpec((None, tm, U_BLK), lambda b, t: (b, t, U_CV)),
                  pl.BlockSpec((tm, LANES), lambda b, t: (t, 0)),
                  pl.BlockSpec((tm, LANES), lambda b, t: (t, 0)),
                  pl.BlockSpec(eye.shape, lambda b, t: (0, 0))],
        out_specs=[pl.BlockSpec((None, tm, H * 256), lambda b, t: (b, t, 0)),
                   pl.BlockSpec((None, tm, H * LANES), lambda b, t: (b, t, 0)),
                   pl.BlockSpec((None, H, None, DIFF_V, tm), lambda b, t: (b, 0, t, 0, 0))],
        out_shape=[jax.ShapeDtypeStruct((B, T, H * 256), BF16),
                   jax.ShapeDtypeStruct((B, T, H * LANES), BF16),
                   jax.ShapeDtypeStruct((B, H, nC, DIFF_V, tm), BF16)],
        compiler_params=_params(2), name="diff_prep")(u, u, u, rc, rs, eye)


def _diff_attn_kernel(*refs, n_lat, tkc, lam_init, n_sub, ts):
    if n_lat:
        q_ref, kc_ref, vtc_ref, kl_ref, vtl_ref, lv_ref, sub_ref, o_ref = refs[:8]
        scratch = refs[8:]
    else:
        q_ref, kc_ref, vtc_ref, lv_ref, sub_ref, o_ref = refs[:6]
        kl_ref = vtl_ref = None
        scratch = refs[6:]
    streams = _split_streams(scratch, 2 * n_sub)
    qs = []
    for sub in range(n_sub):
        qs += [functools.partial(_load_block, q_ref, sub * ts, ts, 0, LANES),
               functools.partial(_load_block, q_ref, sub * ts, ts, LANES, LANES)]
    _attend(qs, kc_ref, vtc_ref, kl_ref, vtl_ref, n_lat, tkc, streams)
    lv = lv_ref[...]
    lam = (jnp.exp(jnp.sum(lv[0:1] * lv[1:2], axis=-1, keepdims=True))
           - jnp.exp(jnp.sum(lv[2:3] * lv[3:4], axis=-1, keepdims=True)) + lam_init)
    for sub in range(n_sub):
        st0, st1 = streams[2 * sub], streams[2 * sub + 1]
        o = st0.acc[...] / st0.l[...] - lam * (st1.acc[...] / st1.l[...])
        ot = o.T
        o_ref[sub * ts:(sub + 1) * ts, :] = (_rms(ot, sub_ref[...]) * (1.0 - lam_init)).astype(o_ref.dtype)


def _diff_attn(q, k_ctx, vt_ctx, k_lat, vt_lat, lam_vecs, subln, lam_init):
    B, Tq, _ = q.shape
    H = DIFF_HEADS
    Tc = k_ctx.shape[1]
    tq = _pick(Tq, DIFF_TQ)
    n_lat = 0 if k_lat is None else vt_lat.shape[2]
    tkc = 0 if k_lat is None else vt_lat.shape[4]
    in_specs = [pl.BlockSpec((None, tq, 256), lambda b, h, i: (b, i, h)),
                pl.BlockSpec((None, Tc, LANES), lambda b, h, i: (b, 0, h)),
                pl.BlockSpec((None, None, None, DIFF_V, Tc), lambda b, h, i: (b, h, 0, 0, 0))]
    args = [q, k_ctx, vt_ctx]
    if n_lat:
        T = k_lat.shape[1]
        in_specs += [pl.BlockSpec((None, T, LANES), lambda b, h, i: (b, 0, h)),
                     pl.BlockSpec((None, None, n_lat, DIFF_V, tkc), lambda b, h, i: (b, h, 0, 0, 0))]
        args += [k_lat, vt_lat]
    in_specs += [pl.BlockSpec((4, DIFF_QK), lambda b, h, i: (0, 0)),
                 pl.BlockSpec((1, DIFF_V), lambda b, h, i: (0, 0))]
    args += [lam_vecs, subln.reshape(1, DIFF_V)]
    ts = _pick(tq, ATTN_SUB)
    n_sub = tq // ts
    scratch = _stream_scratch(ts, tkc, DIFF_V) * (2 * n_sub)
    return pl.pallas_call(
        functools.partial(_diff_attn_kernel, n_lat=n_lat, tkc=tkc, lam_init=lam_init, n_sub=n_sub, ts=ts),
        grid=(B, H, Tq // tq), in_specs=in_specs,
        out_specs=pl.BlockSpec((None, tq, DIFF_V), lambda b, h, i: (b, i, h)),
        out_shape=jax.ShapeDtypeStruct((B, Tq, H * DIFF_V), BF16),
        scratch_shapes=scratch, compiler_params=_params(3), name="diff_attn")(*args)


def _gelu_tanh(x):
    return 0.5 * x * (1.0 + jnp.tanh(np.sqrt(2.0 / np.pi) * (x + 0.044715 * (x * x * x))))


def _rglru_kernel(*refs, rev, tb, nt, final):
    if final:
        (x_ref, xp_ref, xn_ref, cw_ref, cb_ref, wri_ref, bri_ref, lam_ref, h0_ref, gate_ref, hf_ref,
         o_ref, ext_ref, carry_ref) = refs
    else:
        (x_ref, xp_ref, xn_ref, cw_ref, cb_ref, wri_ref, bri_ref, lam_ref, h0_ref,
         o_ref, ext_ref, carry_ref) = refs
    i = pl.program_id(1)
    blk = (nt - 1 - i) if rev else i

    @pl.when(i == 0)
    def _():
        carry_ref[...] = h0_ref[...]

    ext_ref[0:SUBLANES, :] = jnp.where(blk > 0, xp_ref[...], 0.0)
    ext_ref[SUBLANES:SUBLANES + tb, :] = x_ref[...]
    ext_ref[SUBLANES + tb:2 * SUBLANES + tb, :] = jnp.where(blk < nt - 1, xn_ref[...], 0.0)
    left = RG_CONV // 2
    xc = cb_ref[...]
    for tap in range(RG_CONV):
        off = SUBLANES - left + tap
        xc = xc + cw_ref[tap:tap + 1, :] * ext_ref[off:off + tb, :]
    xcb = xc.astype(BF16)

    rs, gs = [], []
    for kb in range(RG_BLOCKS):
        ri = _dot(xcb[:, RG_BLOCK * kb:RG_BLOCK * (kb + 1)], wri_ref[kb])
        rs.append(ri[:, :RG_BLOCK])
        gs.append(ri[:, RG_BLOCK:])
    bri = bri_ref[...]
    r = _sigmoid(jnp.concatenate(rs, axis=1) + bri[0:1])
    ig = _sigmoid(jnp.concatenate(gs, axis=1) + bri[1:2])
    lam = lam_ref[...]
    softplus_neg = jnp.maximum(-lam, 0.0) + jnp.log(1.0 + jnp.exp(-jnp.abs(lam)))
    log_a = -RG_C * r * softplus_neg
    a = jnp.exp(log_a)
    bv = jnp.sqrt(-jnp.tanh(log_a) * (a * a + 1.0)) * (ig * xc)

    row = lax.broadcasted_iota(jnp.int32, a.shape, 0)
    d = 1
    while d < tb:
        if rev:
            keep = row < tb - d
            a_sh = pltpu.roll(a, tb - d, 0)
            b_sh = pltpu.roll(bv, tb - d, 0)
        else:
            keep = row >= d
            a_sh = pltpu.roll(a, d, 0)
            b_sh = pltpu.roll(bv, d, 0)
        bv = a * jnp.where(keep, b_sh, 0.0) + bv
        a = a * jnp.where(keep, a_sh, 1.0)
        d *= 2
    h = a * carry_ref[...] + bv
    last = 0 if rev else tb - 1
    carry_ref[...] = h[last:last + 1, :]
    if final:
        o_ref[...] = ((hf_ref[...] + h) * _gelu_tanh(gate_ref[...])).astype(o_ref.dtype)
    else:
        o_ref[...] = h


def _rglru_pass(u, conv_w, conv_b, wri, bri, lam, h0, hf, rev):
    B, T, _ = u.shape
    W = RG_WIDTH
    tb = _pick(T, 256)
    nt = T // tb
    r8 = tb // SUBLANES
    final = hf is not None

    def tix(i):
        return (nt - 1 - i) if rev else i

    in_specs = [pl.BlockSpec((None, tb, U_BLK), lambda b, i: (b, tix(i), U_BX)),
                pl.BlockSpec((None, SUBLANES, U_BLK),
                             lambda b, i: (b, jnp.maximum(tix(i) * r8 - 1, 0), U_BX)),
                pl.BlockSpec((None, SUBLANES, U_BLK),
                             lambda b, i: (b, jnp.minimum((tix(i) + 1) * r8, T // SUBLANES - 1), U_BX)),
                pl.BlockSpec((RG_CONV, W), lambda b, i: (0, 0)),
                pl.BlockSpec((1, W), lambda b, i: (0, 0)),
                pl.BlockSpec((RG_BLOCKS, RG_BLOCK, 2 * RG_BLOCK), lambda b, i: (0, 0, 0)),
                pl.BlockSpec((2, W), lambda b, i: (0, 0)),
                pl.BlockSpec((1, W), lambda b, i: (0, 0)),
                pl.BlockSpec((None, 1, W), lambda b, i: (b, 0, 0))]
    args = [u, u, u, conv_w, conv_b.reshape(1, W), wri, bri, lam.reshape(1, W), h0]
    if final:
        in_specs += [pl.BlockSpec((None, tb, U_BLK), lambda b, i: (b, tix(i), U_BG)),
                     pl.BlockSpec((None, tb, W), lambda b, i: (b, tix(i), 0))]
        args += [u, hf]
    return pl.pallas_call(
        functools.partial(_rglru_kernel, rev=rev, tb=tb, nt=nt, final=final), grid=(B, nt),
        in_specs=in_specs,
        out_specs=pl.BlockSpec((None, tb, W), lambda b, i: (b, tix(i), 0)),
        out_shape=jax.ShapeDtypeStruct((B, T, W), BF16 if final else F32),
        scratch_shapes=[pltpu.VMEM((tb + 2 * SUBLANES, W), F32), pltpu.VMEM((1, W), F32)],
        compiler_params=_params(2), name="rglru_bwd" if rev else "rglru_fwd")(*args)


HGRN_CHUNK = 128
HGRN_SUB = 8


def _hgrn_chunk(qv, z, v, lb, st, ones, rev):
    C = HGRN_CHUNK
    sg = _sigmoid(z)
    k = (1.0 - lb) * _sigmoid(-z)
    g = jnp.log(lb + (1.0 - lb) * sg)
    row = lax.broadcasted_iota(jnp.int32, (C, LANES), 0)
    col = lax.broadcasted_iota(jnp.int32, (C, C), 1)
    rowc = lax.broadcasted_iota(jnp.int32, (C, C), 0)

    G = g
    d = 1
    while d < C:
        if rev:
            G = G + jnp.where(row < C - d, pltpu.roll(G, C - d, 0), 0.0)
        else:
            G = G + jnp.where(row >= d, pltpu.roll(G, d, 0), 0.0)
        d *= 2
    G = G * LOG2E
    last = 0 if rev else C - 1
    g_last = G[last:last + 1, :]

    qe = (qv * jnp.exp2(G)).astype(BF16)
    o = _dot_nt(qe, st.astype(BF16))
    kdec = (k * jnp.exp2(g_last - G)).astype(BF16)
    vb = v.astype(BF16)
    st_new = st * jnp.exp2(g_last) + _dot(v.T.astype(BF16), kdec)

    pos = (C - 1 - row) if rev else row
    a_sum = jnp.zeros((C, C), F32)
    hsz = HGRN_SUB
    while hsz < C:
        grp = 2 * hsz
        ng = C // grp
        G3 = G.reshape(ng, grp, LANES)
        ref = G3[:, hsz:hsz + 1, :] if rev else G3[:, hsz - 1:hsz, :]
        ref = jnp.broadcast_to(ref, (ng, grp, LANES)).reshape(C, LANES)
        e = jnp.exp2(-jnp.abs(G - ref))
        later = ((pos >> int(np.log2(hsz))) & 1) == 1
        ql = jnp.where(later, qv * e, 0.0).astype(BF16)
        kl = jnp.where(later, 0.0, k * e).astype(BF16)
        al = _dot_nt(ql, kl)
        shift = int(np.log2(grp))
        a_sum = a_sum + jnp.where((rowc >> shift) == (col >> shift), al, 0.0)
        hsz = grp
    o = o + _dot(a_sum.astype(BF16), vb)

    nb = C // HGRN_SUB
    q3 = qv.reshape(nb, HGRN_SUB, LANES)
    k3 = k.reshape(nb, HGRN_SUB, LANES)
    v3 = v.reshape(nb, HGRN_SUB, LANES)
    G3 = G.reshape(nb, HGRN_SUB, LANES)
    rb = lax.broadcasted_iota(jnp.int32, (nb, HGRN_SUB, LANES), 1)
    od = jnp.zeros((nb, HGRN_SUB, LANES), F32)
    for s in range(HGRN_SUB):
        valid = (rb <= s) if rev else (rb >= s)
        e = jnp.exp2(G3 - G3[:, s:s + 1, :])
        p = jnp.where(valid, q3 * k3[:, s:s + 1, :] * e, 0.0)
        rs = _dot(p.reshape(C, LANES).astype(BF16), ones)
        od = od + rs.reshape(nb, HGRN_SUB, LANES) * v3[:, s:s + 1, :]
    return o + od.reshape(C, LANES), st_new


def _hgrn_kernel(*refs, rev, tb, nt, final):
    if final:
        (q_ref, z_ref, v_ref, lb_ref, s0_ref, ones_ref, of_ref, og_ref, gn_ref,
         o_ref, sfin_ref, st_ref) = refs
    else:
        q_ref, z_ref, v_ref, lb_ref, s0_ref, ones_ref, o_ref, sfin_ref, st_ref = refs
    i = pl.program_id(2)

    @pl.when(i == 0)
    def _():
        st_ref[...] = s0_ref[...]

    lb = lb_ref[...]
    ones = ones_ref[...]
    C = HGRN_CHUNK
    nch = tb // C

    def body(c, carry):
        cc = (nch - 1 - c) if rev else c
        r0 = pl.multiple_of(cc * C, C)
        qin = q_ref[pl.ds(r0, C), :]
        qv = qin * _sigmoid(qin)
        o, st_new = _hgrn_chunk(qv, z_ref[pl.ds(r0, C), :], v_ref[pl.ds(r0, C), :], lb,
                                st_ref[...], ones, rev)
        st_ref[...] = st_new
        if final:
            o = o + of_ref[pl.ds(r0, C), :]
            og = og_ref[pl.ds(r0, C), :]
            o = _rms(o, gn_ref[...]) * (og * _sigmoid(og))
        o_ref[pl.ds(r0, C), :] = o.astype(o_ref.dtype)
        return carry

    lax.fori_loop(0, nch, body, 0, unroll=True)

    @pl.when(i == nt - 1)
    def _():
        sfin_ref[...] = st_ref[...]


def _hgrn_pass(u, z_blk, lb, s0, ones, of, g_norm, rev):
    B, T, _ = u.shape
    H = HGRN_HEADS
    tb = _pick(T, 512)
    nt = T // tb
    final = of is not None

    def tix(i):
        return (nt - 1 - i) if rev else i

    def ublk(blk):
        return pl.BlockSpec((None, tb, LANES), lambda b, h, i: (b, tix(i), blk * (U_BLK // LANES) + h))

    in_specs = [ublk(U_DQ), ublk(z_blk), ublk(U_DI),
                pl.BlockSpec((1, LANES), lambda b, h, i: (0, h)),
                pl.BlockSpec((None, None, HGRN_V, HGRN_K), lambda b, h, i: (b, h, 0, 0)),
                pl.BlockSpec((LANES, LANES), lambda b, h, i: (0, 0))]
    args = [u, u, u, lb.reshape(1, -1), s0, ones]
    if final:
        in_specs += [pl.BlockSpec((None, tb, LANES), lambda b, h, i: (b, tix(i), h)),
                     ublk(U_DG),
                     pl.BlockSpec((1, HGRN_V), lambda b, h, i: (0, 0))]
        args += [of, u, g_norm.reshape(1, HGRN_V)]
    return pl.pallas_call(
        functools.partial(_hgrn_kernel, rev=rev, tb=tb, nt=nt, final=final), grid=(B, H, nt),
        in_specs=in_specs,
        out_specs=[pl.BlockSpec((None, tb, LANES), lambda b, h, i: (b, tix(i), h)),
                   pl.BlockSpec((None, None, HGRN_V, HGRN_K), lambda b, h, i: (b, h, 0, 0))],
        out_shape=[jax.ShapeDtypeStruct((B, T, H * HGRN_V), BF16 if final else F32),
                   jax.ShapeDtypeStruct((B, H, HGRN_V, HGRN_K), F32)],
        scratch_shapes=[pltpu.VMEM((HGRN_V, HGRN_K), F32)],
        compiler_params=_params(3), name="hgrn_bwd" if rev else "hgrn_fwd")(*args)


def _merge_kernel(x_ref, mod_ref, pre_ref, post_ref, ya_ref, yb_ref, yc_ref, yd_ref, w_ref, wb_ref, mb_ref,
                  o_ref, h_ref, acc_ref):
    n = pl.program_id(2)

    @pl.when(n == 0)
    def _():
        h_ref[...] = _prenorm(x_ref[...], mod_ref, 1, pre_ref[...]).astype(BF16)
        acc_ref[...] = jnp.zeros(acc_ref.shape, F32)

    for nb, y_ref in enumerate((ya_ref, yb_ref, yc_ref, yd_ref)):
        @pl.when(n == nb)
        def _(y_ref=y_ref):
            gate = _sigmoid(_dot(h_ref[...], w_ref[...]) + mb_ref[...])
            acc_ref[...] += gate * _dot(y_ref[...], wb_ref[...])

    @pl.when(n == N_BRANCH)
    def _():
        mix = _dot(acc_ref[...].astype(BF16), w_ref[...])
        o_ref[...] = x_ref[...] + mod_ref[5:6, :] * _rms(mix, post_ref[...])


def _merge(x, mod, pre_g, post_g, ys, w5, wb, mb, l):
    B, T, D = x.shape
    tm = _pick(T, 512)
    last = N_BRANCH - 1
    yspec = pl.BlockSpec((None, tm, BRANCH_W), lambda b, t, n: (b, t, 0))
    return pl.pallas_call(
        _merge_kernel, grid=(B, T // tm, N_BRANCH + 1),
        in_specs=[pl.BlockSpec((None, tm, D), lambda b, t, n: (b, t, 0)),
                  pl.BlockSpec((None, N_MOD, D), lambda b, t, n: (b, 0, 0)),
                  pl.BlockSpec((1, D), lambda b, t, n: (0, 0)),
                  pl.BlockSpec((1, D), lambda b, t, n: (0, 0)),
                  yspec, yspec, yspec, yspec,
                  pl.BlockSpec((D, D), lambda b, t, n: (0, n)),
                  pl.BlockSpec((None, None, BRANCH_W, D), lambda b, t, n: (l, jnp.minimum(n, last), 0, 0)),
                  pl.BlockSpec((None, None, 1, D), lambda b, t, n: (l, jnp.minimum(n, last), 0, 0))],
        out_specs=pl.BlockSpec((None, tm, D), lambda b, t, n: (b, t, 0)),
        out_shape=jax.ShapeDtypeStruct(x.shape, F32),
        scratch_shapes=[pltpu.VMEM((tm, D), BF16), pltpu.VMEM((tm, D), F32)],
        compiler_params=_params(3), name="merge")(
            x, mod, pre_g.reshape(1, D), post_g.reshape(1, D), *ys, w5, wb, mb)


def _rope_tables(n_tok):
    axis_dim = MLA_ROPE // 2
    inv_freq = ROPE_BASE ** (-jnp.arange(0, axis_dim, 2, dtype=F32) / axis_dim)
    tok = lax.broadcasted_iota(jnp.int32, (n_tok, LANES), 0)
    lane = lax.broadcasted_iota(jnp.int32, (n_tok, LANES), 1)
    pos = jnp.where((lane & 32) == 0, tok // GRID_W, tok % GRID_W).astype(F32)
    ang = pos * jnp.tile(inv_freq, LANES // inv_freq.shape[0])[None, :]
    cos = jnp.cos(ang)
    sin = jnp.where((lane & 16) == 0, -jnp.sin(ang), jnp.sin(ang))
    first = lane < MLA_ROPE
    return cos, sin, jnp.where(first, cos, 1.0), jnp.where(first, sin, 0.0)


def _layer_weights(l, w_in, mla_w_uq, mla_w_ukv, rg_w_r, rg_w_i, rg_b_r, rg_b_i, w_out):
    D = D_MODEL
    cuts = np.cumsum([MLA_Q_LORA, MLA_KV_LORA, MLA_ROPE, RG_WIDTH, RG_WIDTH, 512, 512, 512,
                      512, 512, 512, 512, 512])
    wi = w_in[l]
    parts = jnp.split(wi[:, :cuts[-1]], cuts[:-1].tolist(), axis=1)
    (a_q, a_kv, a_kr, b_x, b_g, c_q, c_k, c_v, d_q, d_ff, d_fb, d_i, d_g) = parts
    pad = jnp.zeros((D, U_BLK - MLA_KV_LORA - MLA_ROPE), wi.dtype)
    w_u = jnp.concatenate([a_q, b_x, b_g, c_q, c_k, c_v, d_q, d_ff, d_fb, d_i, d_g, a_kv, a_kr, pad],
                          axis=1).astype(BF16)
    w5 = jnp.concatenate([wi[:, cuts[-1]:], w_out[l]], axis=1).astype(BF16)

    wq = mla_w_uq[l].reshape(MLA_Q_LORA, MLA_HEADS, MLA_NOPE + MLA_ROPE)
    wq = jnp.pad(wq, ((0, 0), (0, 0), (0, 256 - MLA_NOPE - MLA_ROPE))).reshape(MLA_Q_LORA, -1).astype(BF16)
    wkv = mla_w_ukv[l].reshape(MLA_KV_LORA, MLA_HEADS, MLA_NOPE + MLA_V)
    wk = wkv[:, :, :MLA_NOPE].reshape(MLA_KV_LORA, -1).astype(BF16)
    wvt = wkv[:, :, MLA_NOPE:].reshape(MLA_KV_LORA, -1).T.astype(BF16)

    wri = [jnp.concatenate([rg_w_r[l, d], rg_w_i[l, d]], axis=-1).astype(BF16) for d in range(2)]
    bri = [jnp.stack([rg_b_r[l, d], rg_b_i[l, d]], axis=0) for d in range(2)]
    return w_u, w5, wq, wk, wvt, wri, bri


def kernel(x, c, ctx, c_ctx, ada_w, ada_b, pre_norm, post_norm, ffn_w_gate, ffn_w_up, ffn_w_down, w_in, mla_q_norm, mla_w_uq, mla_kv_norm, mla_w_ukv, rg_conv_w, rg_conv_b, rg_w_r, rg_b_r, rg_w_i, rg_b_i, rg_lambda, diff_lambda, diff_subln, hgrn_lb_logits, hgrn_norm, merge_b, w_branch, w_out):
    B, S, D = x.shape
    Tc = ctx.shape[1]
    depth = ada_w.shape[0]
    tkc_mla = _pick(S, MLA_TKC)
    tkc_diff = _pick(S, DIFF_TKC)

    rc_diff, rs_diff, rc_mla, rs_mla = _rope_tables(S)
    rc_ctx = jnp.ones((Tc, LANES), F32)
    rs_ctx = jnp.zeros((Tc, LANES), F32)

    lb_soft = jax.nn.softmax(hgrn_lb_logits.astype(F32), axis=0)
    hgrn_lb = jnp.cumsum(lb_soft, axis=0) - lb_soft[0]

    c8 = jnp.zeros((SUBLANES, D), F32).at[:B].set(c).at[B].set(c_ctx)
    eye = jnp.eye(DIFF_HEADS * DIFF_V, dtype=BF16)
    ones = jnp.ones((LANES, LANES), BF16)
    zero_state = jnp.zeros((B, HGRN_HEADS, HGRN_V, HGRN_K), F32)
    zero_h = jnp.zeros((B, 1, RG_WIDTH), F32)

    wg = ffn_w_gate.astype(BF16)
    wu = ffn_w_up.astype(BF16)
    wd = ffn_w_down.astype(BF16)
    wb = w_branch.astype(BF16)
    mb = merge_b.reshape(depth, N_BRANCH, 1, D)

    xc = ctx
    for l in range(depth):
        need_ctx = l < depth - 1
        lam_init = 0.8 - 0.6 * float(np.exp(-0.3 * l))
        mod8 = _ada_mod(c8, ada_w, ada_b, l).reshape(SUBLANES, N_MOD, D)
        mod = mod8[:B]
        mod_c = jnp.broadcast_to(mod8[B:B + 1], (B, N_MOD, D))
        w_u, w5, wq, wk, wvt, wri, bri = _layer_weights(
            l, w_in, mla_w_uq, mla_w_ukv, rg_w_r, rg_w_i, rg_b_r, rg_b_i, w_out)

        x = _ffn(x, mod, 0, pre_norm[l, 0], post_norm[l, 0], wg, wu, wd, l, 0)
        xc = _ffn(xc, mod_c, 0, pre_norm[l, 0], post_norm[l, 0], wg, wu, wd, l, 0)

        u = _inproj(x, mod, pre_norm[l, 1], w_u)
        uc = _inproj(xc, mod_c, pre_norm[l, 1], w_u)

        qa_c, ka_c, vta_c = _mla_prep(uc, rc_ctx, rs_ctx, mla_q_norm[l], mla_kv_norm[l], wq, wk, wvt, Tc, False)
        qa, ka, vta = _mla_prep(u, rc_mla, rs_mla, mla_q_norm[l], mla_kv_norm[l], wq, wk, wvt, tkc_mla, True)
        y_a = _mla_attn(qa, ka_c, vta_c, ka, vta)

        qd_c, kd_c, vtd_c = _diff_prep(uc, rc_ctx, rs_ctx, eye, Tc, False)
        qd, kd, vtd = _diff_prep(u, rc_diff, rs_diff, eye, tkc_diff, True)
        y_c = _diff_attn(qd, kd_c, vtd_c, kd, vtd, diff_lambda[l], diff_subln[l], lam_init)

        hc_f = _rglru_pass(uc, rg_conv_w[l], rg_conv_b[l], wri[0], bri[0], rg_lambda[l, 0], zero_h, None, False)
        hl_f = _rglru_pass(u, rg_conv_w[l], rg_conv_b[l], wri[0], bri[0], rg_lambda[l, 0],
                           hc_f[:, Tc - 1:Tc, :], None, False)
        if need_ctx:
            y_b_c = _rglru_pass(uc, rg_conv_w[l], rg_conv_b[l], wri[1], bri[1], rg_lambda[l, 1], zero_h, hc_f, True)
        hc_b = _rglru_pass(uc, rg_conv_w[l], rg_conv_b[l], wri[1], bri[1], rg_lambda[l, 1], zero_h, None, True)
        y_b = _rglru_pass(u, rg_conv_w[l], rg_conv_b[l], wri[1], bri[1], rg_lambda[l, 1],
                          hc_b[:, 0:1, :], hl_f, True)

        oc_f, s_f = _hgrn_pass(uc, U_DFF, hgrn_lb[l], zero_state, ones, None, None, False)
        if need_ctx:
            y_d_c, s_b = _hgrn_pass(uc, U_DFB, hgrn_lb[l], zero_state, ones, oc_f, hgrn_norm[l], True)
        else:
            _, s_b = _hgrn_pass(uc, U_DFB, hgrn_lb[l], zero_state, ones, None, None, True)
        o_f, _ = _hgrn_pass(u, U_DFF, hgrn_lb[l], s_f, ones, None, None, False)
        y_d, _ = _hgrn_pass(u, U_DFB, hgrn_lb[l], s_b, ones, o_f, hgrn_norm[l], True)

        x = _merge(x, mod, pre_norm[l, 1], post_norm[l, 1], (y_a, y_b, y_c, y_d), w5, wb, mb, l)
        if need_ctx:
            y_a_c = _mla_attn(qa_c, ka_c, vta_c, None, None)
            y_c_c = _diff_attn(qd_c, kd_c, vtd_c, None, None, diff_lambda[l], diff_subln[l], lam_init)
            xc = _merge(xc, mod_c, pre_norm[l, 1], post_norm[l, 1], (y_a_c, y_b_c, y_c_c, y_d_c), w5, wb, mb, l)
            xc = _ffn(xc, mod_c, 2, pre_norm[l, 2], post_norm[l, 2], wg, wu, wd, l, 1)

        x = _ffn(x, mod, 2, pre_norm[l, 2], post_norm[l, 2], wg, wu, wd, l, 1)
    return x
```

```python
import functools
from typing import Any, NamedTuple

import jax
import jax.numpy as jnp
import numpy as np
from jax import lax
from jax.experimental import pallas as pl
from jax.experimental.pallas import tpu as pltpu

F32 = jnp.float32
BF16 = jnp.bfloat16

D_MODEL = 2048
GRID_W = 64
N_SUB = 3
N_MOD = 3 * N_SUB
FFN_RES = 0.5
NORM_EPS = 1e-6
ROPE_BASE = 10000.0

MLA_HEADS = 4
MLA_Q_LORA = 512
MLA_KV_LORA = 256
MLA_NOPE = 128
MLA_ROPE = 64
MLA_V = 128
MLA_SCALE = (MLA_NOPE + MLA_ROPE) ** -0.5

RG_WIDTH = 512
RG_BLOCKS = 4
RG_BLOCK = RG_WIDTH // RG_BLOCKS
RG_CONV = 4
RG_C = 8.0

DIFF_HEADS = 4
DIFF_QK = 64
DIFF_V = 2 * DIFF_QK
DIFF_SCALE = DIFF_QK ** -0.5

HGRN_HEADS = 4
HGRN_K = 128
HGRN_V = 128

N_BRANCH = 4
BRANCH_W = 512

LANES = 128
SUBLANES = 8
VMEM_LIMIT_BYTES = 56 * 1024 * 1024

NT_DIMS = (((1,), (1,)), ((), ()))


def _params(n_axes):
    return pltpu.CompilerParams(dimension_semantics=("arbitrary",) * n_axes,
                                vmem_limit_bytes=VMEM_LIMIT_BYTES)


def _pick(n, pref):
    t = min(n, pref)
    assert n % t == 0, (n, pref)
    return t


def _rms(x, g):
    return x * lax.rsqrt(jnp.mean(x * x, axis=-1, keepdims=True) + NORM_EPS) * g


def _sigmoid(x):
    return jax.nn.sigmoid(x)


def _dot(a, b):
    return jnp.dot(a, b, preferred_element_type=F32)


def _dot_nt(a, b):
    return lax.dot_general(a, b, NT_DIMS, preferred_element_type=F32)


def _ada_kernel(c_ref, w_ref, b_ref, o_ref):
    c = c_ref[...]
    cond = (c * _sigmoid(c)).astype(BF16)
    o_ref[...] = _dot(cond, w_ref[...].astype(BF16)) + b_ref[...]


def _ada_mod(c8, w, b, l):
    L, D, N = w.shape
    tn = _pick(N, 2048)
    return pl.pallas_call(
        _ada_kernel, grid=(N // tn,),
        in_specs=[pl.BlockSpec((SUBLANES, D), lambda n: (0, 0)),
                  pl.BlockSpec((None, D, tn), lambda n: (l, 0, n)),
                  pl.BlockSpec((None, 1, tn), lambda n: (l, 0, n))],
        out_specs=pl.BlockSpec((SUBLANES, tn), lambda n: (0, n)),
        out_shape=jax.ShapeDtypeStruct((SUBLANES, N), F32),
        compiler_params=_params(1), name="ada_mod")(c8, w, b.reshape(L, 1, N))


def _prenorm(x, mod_ref, j, g):
    return _rms(x, g) * (1.0 + mod_ref[3 * j + 1:3 * j + 2, :]) + mod_ref[3 * j:3 * j + 1, :]


def _tile_slot(nt):
    tile = pl.program_id(0) * nt + pl.program_id(1)
    return tile == 0, lax.rem(tile, 2)


def _next_tile(b, t, nb, nt):
    wrap = t + 1 >= nt
    return jnp.where(wrap, jnp.minimum(b + 1, nb - 1), b), jnp.where(wrap, jnp.where(b + 1 < nb, 0, t), t + 1)


def _ffn_kernel(x_ref, xn_ref, mod_ref, modn_ref, pre_ref, post_ref, wg_ref, wu_ref, wd_ref, o_ref,
                h_ref, acc_ref, *, j, nf, nt):
    f = pl.program_id(2)
    first, slot = _tile_slot(nt)

    @pl.when(jnp.logical_and(f == 0, first))
    def _():
        h_ref[0] = _prenorm(x_ref[...], mod_ref, j, pre_ref[...]).astype(BF16)

    @pl.when(f == 0)
    def _():
        acc_ref[...] = jnp.zeros(acc_ref.shape, F32)

    def step():
        h = h_ref[slot]
        g = _dot(h, wg_ref[...])
        u = _dot(h, wu_ref[...])
        a = (g * _sigmoid(g) * u).astype(BF16)
        acc_ref[...] += _dot(a, wd_ref[...])

    @pl.when(f < nf - 1)
    def _():
        step()

    @pl.when(f == nf - 1)
    def _():
        step()
        h_ref[1 - slot] = _prenorm(xn_ref[...], modn_ref, j, pre_ref[...]).astype(BF16)
        yn = _rms(acc_ref[...], post_ref[...])
        o_ref[...] = x_ref[...] + FFN_RES * mod_ref[3 * j + 2:3 * j + 3, :] * yn


def _ffn(x, mod, j, pre_g, post_g, wg, wu, wd, l, i):
    B, T, D = x.shape
    F = wg.shape[-1]
    tm = _pick(T, 512)
    tf = _pick(F, 512)
    nf = F // tf
    nt = T // tm

    def nxt(b, t):
        return _next_tile(b, t, B, nt)

    return pl.pallas_call(
        functools.partial(_ffn_kernel, j=j, nf=nf, nt=nt), grid=(B, nt, nf),
        in_specs=[pl.BlockSpec((None, tm, D), lambda b, t, f: (b, t, 0)),
                  pl.BlockSpec((None, tm, D), lambda b, t, f: (*nxt(b, t), 0)),
                  pl.BlockSpec((None, N_MOD, D), lambda b, t, f: (b, 0, 0)),
                  pl.BlockSpec((None, N_MOD, D), lambda b, t, f: (nxt(b, t)[0], 0, 0)),
                  pl.BlockSpec((1, D), lambda b, t, f: (0, 0)),
                  pl.BlockSpec((1, D), lambda b, t, f: (0, 0)),
                  pl.BlockSpec((None, None, D, tf), lambda b, t, f: (l, i, 0, f)),
                  pl.BlockSpec((None, None, D, tf), lambda b, t, f: (l, i, 0, f)),
                  pl.BlockSpec((None, None, tf, D), lambda b, t, f: (l, i, f, 0))],
        out_specs=pl.BlockSpec((None, tm, D), lambda b, t, f: (b, t, 0)),
        out_shape=jax.ShapeDtypeStruct(x.shape, F32),
        scratch_shapes=[pltpu.VMEM((2, tm, D), BF16), pltpu.VMEM((tm, D), F32)],
        compiler_params=_params(3), name="ffn")(
            x, x, mod, mod, pre_g.reshape(1, D), post_g.reshape(1, D), wg, wu, wd)


def _inproj_kernel(x_ref, xn_ref, mod_ref, modn_ref, pre_ref, w_ref, o_ref, h_ref, *, nn, nt):
    n = pl.program_id(2)
    first, slot = _tile_slot(nt)

    @pl.when(jnp.logical_and(n == 0, first))
    def _():
        h_ref[0] = _prenorm(x_ref[...], mod_ref, 1, pre_ref[...]).astype(BF16)

    @pl.when(n < nn - 1)
    def _():
        o_ref[...] = _dot(h_ref[slot], w_ref[...])

    @pl.when(n == nn - 1)
    def _():
        o_ref[...] = _dot(h_ref[slot], w_ref[...])
        h_ref[1 - slot] = _prenorm(xn_ref[...], modn_ref, 1, pre_ref[...]).astype(BF16)


def _inproj(x, mod, pre_g, w):
    B, T, D = x.shape
    N = w.shape[1]
    tm = _pick(T, 512)
    tn = _pick(N, 1024)
    nt = T // tm
    nn = N // tn

    def nxt(b, t):
        return _next_tile(b, t, B, nt)

    return pl.pallas_call(
        functools.partial(_inproj_kernel, nn=nn, nt=nt), grid=(B, nt, nn),
        in_specs=[pl.BlockSpec((None, tm, D), lambda b, t, n: (b, t, 0)),
                  pl.BlockSpec((None, tm, D), lambda b, t, n: (*nxt(b, t), 0)),
                  pl.BlockSpec((None, N_MOD, D), lambda b, t, n: (b, 0, 0)),
                  pl.BlockSpec((None, N_MOD, D), lambda b, t, n: (nxt(b, t)[0], 0, 0)),
                  pl.BlockSpec((1, D), lambda b, t, n: (0, 0)),
                  pl.BlockSpec((D, tn), lambda b, t, n: (0, n))],
        out_specs=pl.BlockSpec((None, tm, tn), lambda b, t, n: (b, t, n)),
        out_shape=jax.ShapeDtypeStruct((B, T, N), F32),
        scratch_shapes=[pltpu.VMEM((2, tm, D), BF16)],
        compiler_params=_params(3), name="inproj")(x, x, mod, mod, pre_g.reshape(1, D), w)


U_AQ, U_BX, U_BG, U_CQ, U_CK, U_CV, U_DQ, U_DFF, U_DFB, U_DI, U_DG, U_AKV = range(12)
U_BLK = 512


def _rope128(x, c, s):
    lane = lax.broadcasted_iota(jnp.int32, x.shape, 1)
    partner = jnp.where((lane & 31) < 16, pltpu.roll(x, LANES - 16, 1), pltpu.roll(x, 16, 1))
    return x * c + partner * s


def _mla_prep_kernel(uq_ref, ukv_ref, c_ref, s_ref, qn_ref, kvn_ref, wq_ref, wk_ref, wvt_ref,
                     q_ref, k_ref, vt_ref, *, rope):
    uqn = _rms(uq_ref[...], qn_ref[...]).astype(BF16)
    q = _dot(uqn, wq_ref[...]) * (MLA_SCALE * LOG2E)
    ukv = ukv_ref[...]
    ukvn = _rms(ukv[:, :MLA_KV_LORA], kvn_ref[...]).astype(BF16)
    kn = _dot(ukvn, wk_ref[...])
    kr = ukv[:, MLA_KV_LORA:MLA_KV_LORA + LANES]
    if rope:
        c = c_ref[...]
        s = s_ref[...]
        kr = _rope128(kr, c, s)
    krb = kr.astype(BF16)
    for h in range(MLA_HEADS):
        qr = q[:, 256 * h + 128:256 * h + 256]
        if rope:
            qr = _rope128(qr, c, s)
        q_ref[:, 256 * h:256 * h + 128] = q[:, 256 * h:256 * h + 128].astype(BF16)
        q_ref[:, 256 * h + 128:256 * h + 256] = qr.astype(BF16)
        k_ref[:, 256 * h:256 * h + 128] = kn[:, 128 * h:128 * h + 128].astype(BF16)
        k_ref[:, 256 * h + 128:256 * h + 256] = krb
    vt = _dot_nt(wvt_ref[...], ukvn)
    tm = vt.shape[1]
    vt_ref[...] = vt.reshape(MLA_HEADS, MLA_V, tm).astype(BF16)


def _mla_prep(u, rc, rs, q_norm, kv_norm, wq, wk, wvt, tkc, rope):
    B, T, _ = u.shape
    tm = tkc
    nC = T // tm
    H = MLA_HEADS
    return pl.pallas_call(
        functools.partial(_mla_prep_kernel, rope=rope), grid=(B, nC),
        in_specs=[pl.BlockSpec((None, tm, U_BLK), lambda b, t: (b, t, U_AQ)),
                  pl.BlockSpec((None, tm, U_BLK), lambda b, t: (b, t, U_AKV)),
                  pl.BlockSpec((tm, LANES), lambda b, t: (t, 0)),
                  pl.BlockSpec((tm, LANES), lambda b, t: (t, 0)),
                  pl.BlockSpec((1, MLA_Q_LORA), lambda b, t: (0, 0)),
                  pl.BlockSpec((1, MLA_KV_LORA), lambda b, t: (0, 0)),
                  pl.BlockSpec(wq.shape, lambda b, t: (0, 0)),
                  pl.BlockSpec(wk.shape, lambda b, t: (0, 0)),
                  pl.BlockSpec(wvt.shape, lambda b, t: (0, 0))],
        out_specs=[pl.BlockSpec((None, tm, H * 256), lambda b, t: (b, t, 0)),
                   pl.BlockSpec((None, tm, H * 256), lambda b, t: (b, t, 0)),
                   pl.BlockSpec((None, H, None, MLA_V, tm), lambda b, t: (b, 0, t, 0, 0))],
        out_shape=[jax.ShapeDtypeStruct((B, T, H * 256), BF16),
                   jax.ShapeDtypeStruct((B, T, H * 256), BF16),
                   jax.ShapeDtypeStruct((B, H, nC, MLA_V, tm), BF16)],
        compiler_params=_params(2), name="mla_prep")(
            u, u, rc, rs, q_norm.reshape(1, -1), kv_norm.reshape(1, -1), wq, wk, wvt)


ATTN_SUB = 512
MLA_TQ = 1024
DIFF_TQ = 512
MLA_TKC = 1024
DIFF_TKC = 1024
LOG2E = float(np.log2(np.e))


class _Stream(NamedTuple):
    m: Any
    l: Any
    acc: Any
    s: Any = None
    cm: Any = None
    p: Any = None
    al: Any = None


def _stream_scratch(tq, tkc, dv):
    shapes = [pltpu.VMEM((1, tq), F32), pltpu.VMEM((1, tq), F32), pltpu.VMEM((dv, tq), F32)]
    if tkc:
        shapes += [pltpu.VMEM((2, tkc, tq), F32), pltpu.VMEM((2, 1, tq), F32),
                   pltpu.VMEM((2, tkc, tq), BF16), pltpu.VMEM((2, 1, tq), F32)]
    return shapes


def _load_block(ref, r0, nr, c0, nc):
    return ref[r0:r0 + nr, c0:c0 + nc]


def _softmax_step(k, vt, q, st):
    s = _dot_nt(k(), q())
    m_prev = st.m[...]
    m_new = jnp.maximum(m_prev, jnp.max(s, axis=0, keepdims=True))
    alpha = jnp.exp2(m_prev - m_new)
    p = jnp.exp2(s - m_new)
    st.l[...] = alpha * st.l[...] + jnp.sum(p, axis=0, keepdims=True)
    st.acc[...] = alpha * st.acc[...] + _dot(vt(), p.astype(BF16))
    st.m[...] = m_new


def _stage_scores(k, qs, streams, slot):
    for q, st in zip(qs, streams):
        s = _dot_nt(k(), q())
        st.s[slot] = s
        st.cm[slot] = jnp.max(s, axis=0, keepdims=True)


def _stage_softmax(streams, slot):
    for st in streams:
        m_prev = st.m[...]
        m_new = jnp.maximum(m_prev, st.cm[slot])
        alpha = jnp.exp2(m_prev - m_new)
        p = jnp.exp2(st.s[slot] - m_new)
        st.l[...] = alpha * st.l[...] + jnp.sum(p, axis=0, keepdims=True)
        st.p[slot] = p.astype(BF16)
        st.al[slot] = alpha
        st.m[...] = m_new


def _stage_values(vt, streams, slot):
    for st in streams:
        st.acc[...] = st.al[slot] * st.acc[...] + _dot(vt(), st.p[slot])


def _attend(qs, kc_ref, vtc_ref, kl_ref, vtl_ref, n_lat, tkc, streams):
    for st in streams:
        st.m[...] = jnp.full(st.m.shape, -jnp.inf, F32)
        st.l[...] = jnp.zeros(st.l.shape, F32)
        st.acc[...] = jnp.zeros(st.acc.shape, F32)
    for q, st in zip(qs, streams):
        _softmax_step(lambda: kc_ref[...], lambda: vtc_ref[...], q, st)
    if not n_lat:
        return

    def keys(j):
        return lambda: kl_ref[pl.ds(pl.multiple_of(j * tkc, tkc), tkc), :]

    def vals(j):
        return lambda: vtl_ref[j]

    if n_lat < 4 or n_lat % 2:
        def plain(j, carry):
            for q, st in zip(qs, streams):
                _softmax_step(keys(j), vals(j), q, st)
            return carry
        lax.fori_loop(0, n_lat, plain, 0)
        return

    _stage_scores(keys(0), qs, streams, 0)
    _stage_scores(keys(1), qs, streams, 1)
    _stage_softmax(streams, 0)

    def body(i, carry):
        j = 2 * i
        _stage_scores(keys(j), qs, streams, 0)
        _stage_softmax(streams, 1)
        _stage_values(vals(j - 2), streams, 0)
        _stage_scores(keys(j + 1), qs, streams, 1)
        _stage_softmax(streams, 0)
        _stage_values(vals(j - 1), streams, 1)
        return carry

    lax.fori_loop(1, n_lat // 2, body, 0)
    _stage_softmax(streams, 1)
    _stage_values(vals(n_lat - 2), streams, 0)
    _stage_values(vals(n_lat - 1), streams, 1)


def _split_streams(scratch, n):
    per = len(scratch) // n
    return [_Stream(*scratch[i * per:(i + 1) * per]) for i in range(n)]


def _mla_attn_kernel(*refs, n_lat, tkc, n_sub, ts):
    if n_lat:
        q_ref, kc_ref, vtc_ref, kl_ref, vtl_ref, o_ref = refs[:6]
        scratch = refs[6:]
    else:
        q_ref, kc_ref, vtc_ref, o_ref = refs[:4]
        kl_ref = vtl_ref = None
        scratch = refs[4:]
    streams = _split_streams(scratch, n_sub)
    qs = [functools.partial(_load_block, q_ref, sub * ts, ts, 0, q_ref.shape[1]) for sub in range(n_sub)]
    _attend(qs, kc_ref, vtc_ref, kl_ref, vtl_ref, n_lat, tkc, streams)
    for sub, st in enumerate(streams):
        o = st.acc[...] / st.l[...]
        o_ref[sub * ts:(sub + 1) * ts, :] = o.T.astype(o_ref.dtype)


def _mla_attn(q, k_ctx, vt_ctx, k_lat, vt_lat):
    B, Tq, _ = q.shape
    H = MLA_HEADS
    Tc = k_ctx.shape[1]
    tq = _pick(Tq, MLA_TQ)
    n_lat = 0 if k_lat is None else vt_lat.shape[2]
    tkc = 0 if k_lat is None else vt_lat.shape[4]
    in_specs = [pl.BlockSpec((None, tq, 256), lambda b, h, i: (b, i, h)),
                pl.BlockSpec((None, Tc, 256), lambda b, h, i: (b, 0, h)),
                pl.BlockSpec((None, None, None, MLA_V, Tc), lambda b, h, i: (b, h, 0, 0, 0))]
    args = [q, k_ctx, vt_ctx]
    if n_lat:
        T = k_lat.shape[1]
        in_specs += [pl.BlockSpec((None, T, 256), lambda b, h, i: (b, 0, h)),
                     pl.BlockSpec((None, None, n_lat, MLA_V, tkc), lambda b, h, i: (b, h, 0, 0, 0))]
        args += [k_lat, vt_lat]
    ts = _pick(tq, ATTN_SUB)
    n_sub = tq // ts
    return pl.pallas_call(
        functools.partial(_mla_attn_kernel, n_lat=n_lat, tkc=tkc, n_sub=n_sub, ts=ts),
        grid=(B, H, Tq // tq), in_specs=in_specs,
        out_specs=pl.BlockSpec((None, tq, MLA_V), lambda b, h, i: (b, i, h)),
        out_shape=jax.ShapeDtypeStruct((B, Tq, H * MLA_V), BF16),
        scratch_shapes=_stream_scratch(ts, tkc, MLA_V) * n_sub,
        compiler_params=_params(3), name="mla_attn")(*args)


def _diff_prep_kernel(uq_ref, uk_ref, uv_ref, c_ref, s_ref, eye_ref, q_ref, k_ref, vt_ref, *, rope):
    uq = uq_ref[...] * (DIFF_SCALE * LOG2E)
    uk = uk_ref[...]
    lane = lax.broadcasted_iota(jnp.int32, (uq.shape[0], LANES), 1)
    if rope:
        c = c_ref[...]
        s = s_ref[...]
    for h in range(DIFF_HEADS):
        qh = uq[:, LANES * h:LANES * (h + 1)]
        kh = uk[:, LANES * h:LANES * (h + 1)]
        if rope:
            qh = _rope128(qh, c, s)
            kh = _rope128(kh, c, s)
        q_ref[:, 256 * h:256 * h + 128] = jnp.where(lane < DIFF_QK, qh, 0.0).astype(BF16)
        q_ref[:, 256 * h + 128:256 * h + 256] = jnp.where(lane < DIFF_QK, 0.0, qh).astype(BF16)
        k_ref[:, LANES * h:LANES * (h + 1)] = kh.astype(BF16)
    vt = _dot_nt(eye_ref[...], uv_ref[...].astype(BF16))
    tm = vt.shape[1]
    vt_ref[...] = vt.reshape(DIFF_HEADS, DIFF_V, tm).astype(BF16)


def _diff_prep(u, rc, rs, eye, tkc, rope):
    B, T, _ = u.shape
    tm = tkc
    nC = T // tm
    H = DIFF_HEADS
    return pl.pallas_call(
        functools.partial(_diff_prep_kernel, rope=rope), grid=(B, nC),
        in_specs=[pl.BlockSpec((None, tm, U_BLK), lambda b, t: (b, t, U_CQ)),
                  pl.BlockSpec((None, tm, U_BLK), lambda b, t: (b, t, U_CK)),
                  pl.BlockSpec((None, tm, U_BLK), lambda b, t: (b, t, U_CV)),
                  pl.BlockSpec((tm, LANES), lambda b, t: (t, 0)),
                  pl.BlockSpec((tm, LANES), lambda b, t: (t, 0)),
                  pl.BlockSpec(eye.shape, lambda b, t: (0, 0))],
        out_specs=[pl.BlockSpec((None, tm, H * 256), lambda b, t: (b, t, 0)),
                   pl.BlockSpec((None, tm, H * LANES), lambda b, t: (b, t, 0)),
                   pl.BlockSpec((None, H, None, DIFF_V, tm), lambda b, t: (b, 0, t, 0, 0))],
        out_shape=[jax.ShapeDtypeStruct((B, T, H * 256), BF16),
                   jax.ShapeDtypeStruct((B, T, H * LANES), BF16),
                   jax.ShapeDtypeStruct((B, H, nC, DIFF_V, tm), BF16)],
        compiler_params=_params(2), name="diff_prep")(u, u, u, rc, rs, eye)


def _diff_attn_kernel(*refs, n_lat, tkc, lam_init, n_sub, ts):
    if n_lat:
        q_ref, kc_ref, vtc_ref, kl_ref, vtl_ref, lv_ref, sub_ref, o_ref = refs[:8]
        scratch = refs[8:]
    else:
        q_ref, kc_ref, vtc_ref, lv_ref, sub_ref, o_ref = refs[:6]
        kl_ref = vtl_ref = None
        scratch = refs[6:]
    streams = _split_streams(scratch, 2 * n_sub)
    qs = []
    for sub in range(n_sub):
        qs += [functools.partial(_load_block, q_ref, sub * ts, ts, 0, LANES),
               functools.partial(_load_block, q_ref, sub * ts, ts, LANES, LANES)]
    _attend(qs, kc_ref, vtc_ref, kl_ref, vtl_ref, n_lat, tkc, streams)
    lv = lv_ref[...]
    lam = (jnp.exp(jnp.sum(lv[0:1] * lv[1:2], axis=-1, keepdims=True))
           - jnp.exp(jnp.sum(lv[2:3] * lv[3:4], axis=-1, keepdims=True)) + lam_init)
    for sub in range(n_sub):
        st0, st1 = streams[2 * sub], streams[2 * sub + 1]
        o = st0.acc[...] / st0.l[...] - lam * (st1.acc[...] / st1.l[...])
        ot = o.T
        o_ref[sub * ts:(sub + 1) * ts, :] = (_rms(ot, sub_ref[...]) * (1.0 - lam_init)).astype(o_ref.dtype)


def _diff_attn(q, k_ctx, vt_ctx, k_lat, vt_lat, lam_vecs, subln, lam_init):
    B, Tq, _ = q.shape
    H = DIFF_HEADS
    Tc = k_ctx.shape[1]
    tq = _pick(Tq, DIFF_TQ)
    n_lat = 0 if k_lat is None else vt_lat.shape[2]
    tkc = 0 if k_lat is None else vt_lat.shape[4]
    in_specs = [pl.BlockSpec((None, tq, 256), lambda b, h, i: (b, i, h)),
                pl.BlockSpec((None, Tc, LANES), lambda b, h, i: (b, 0, h)),
                pl.BlockSpec((None, None, None, DIFF_V, Tc), lambda b, h, i: (b, h, 0, 0, 0))]
    args = [q, k_ctx, vt_ctx]
    if n_lat:
        T = k_lat.shape[1]
        in_specs += [pl.BlockSpec((None, T, LANES), lambda b, h, i: (b, 0, h)),
                     pl.BlockSpec((None, None, n_lat, DIFF_V, tkc), lambda b, h, i: (b, h, 0, 0, 0))]
        args += [k_lat, vt_lat]
    in_specs += [pl.BlockSpec((4, DIFF_QK), lambda b, h, i: (0, 0)),
                 pl.BlockSpec((1, DIFF_V), lambda b, h, i: (0, 0))]
    args += [lam_vecs, subln.reshape(1, DIFF_V)]
    ts = _pick(tq, ATTN_SUB)
    n_sub = tq // ts
    scratch = _stream_scratch(ts, tkc, DIFF_V) * (2 * n_sub)
    return pl.pallas_call(
        functools.partial(_diff_attn_kernel, n_lat=n_lat, tkc=tkc, lam_init=lam_init, n_sub=n_sub, ts=ts),
        grid=(B, H, Tq // tq), in_specs=in_specs,
        out_specs=pl.BlockSpec((None, tq, DIFF_V), lambda b, h, i: (b, i, h)),
        out_shape=jax.ShapeDtypeStruct((B, Tq, H * DIFF_V), BF16),
        scratch_shapes=scratch, compiler_params=_params(3), name="diff_attn")(*args)


def _gelu_tanh(x):
    return 0.5 * x * (1.0 + jnp.tanh(np.sqrt(2.0 / np.pi) * (x + 0.044715 * (x * x * x))))


def _rglru_kernel(*refs, rev, tb, nt, final):
    if final:
        (x_ref, xp_ref, xn_ref, cw_ref, cb_ref, wri_ref, bri_ref, lam_ref, h0_ref, gate_ref, hf_ref,
         o_ref, ext_ref, carry_ref) = refs
    else:
        (x_ref, xp_ref, xn_ref, cw_ref, cb_ref, wri_ref, bri_ref, lam_ref, h0_ref,
         o_ref, ext_ref, carry_ref) = refs
    i = pl.program_id(1)
    blk = (nt - 1 - i) if rev else i

    @pl.when(i == 0)
    def _():
        carry_ref[...] = h0_ref[...]

    ext_ref[0:SUBLANES, :] = jnp.where(blk > 0, xp_ref[...], 0.0)
    ext_ref[SUBLANES:SUBLANES + tb, :] = x_ref[...]
    ext_ref[SUBLANES + tb:2 * SUBLANES + tb, :] = jnp.where(blk < nt - 1, xn_ref[...], 0.0)
    left = RG_CONV // 2
    xc = cb_ref[...]
    for tap in range(RG_CONV):
        off = SUBLANES - left + tap
        xc = xc + cw_ref[tap:tap + 1, :] * ext_ref[off:off + tb, :]
    xcb = xc.astype(BF16)

    rs, gs = [], []
    for kb in range(RG_BLOCKS):
        ri = _dot(xcb[:, RG_BLOCK * kb:RG_BLOCK * (kb + 1)], wri_ref[kb])
        rs.append(ri[:, :RG_BLOCK])
        gs.append(ri[:, RG_BLOCK:])
    bri = bri_ref[...]
    r = _sigmoid(jnp.concatenate(rs, axis=1) + bri[0:1])
    ig = _sigmoid(jnp.concatenate(gs, axis=1) + bri[1:2])
    lam = lam_ref[...]
    softplus_neg = jnp.maximum(-lam, 0.0) + jnp.log(1.0 + jnp.exp(-jnp.abs(lam)))
    log_a = -RG_C * r * softplus_neg
    a = jnp.exp(log_a)
    bv = jnp.sqrt(-jnp.tanh(log_a) * (a * a + 1.0)) * (ig * xc)

    row = lax.broadcasted_iota(jnp.int32, a.shape, 0)
    d = 1
    while d < tb:
        if rev:
            keep = row < tb - d
            a_sh = pltpu.roll(a, tb - d, 0)
            b_sh = pltpu.roll(bv, tb - d, 0)
        else:
            keep = row >= d
            a_sh = pltpu.roll(a, d, 0)
            b_sh = pltpu.roll(bv, d, 0)
        bv = a * jnp.where(keep, b_sh, 0.0) + bv
        a = a * jnp.where(keep, a_sh, 1.0)
        d *= 2
    h = a * carry_ref[...] + bv
    last = 0 if rev else tb - 1
    carry_ref[...] = h[last:last + 1, :]
    if final:
        o_ref[...] = ((hf_ref[...] + h) * _gelu_tanh(gate_ref[...])).astype(o_ref.dtype)
    else:
        o_ref[...] = h


def _rglru_pass(u, conv_w, conv_b, wri, bri, lam, h0, hf, rev):
    B, T, _ = u.shape
    W = RG_WIDTH
    tb = _pick(T, 256)
    nt = T // tb
    r8 = tb // SUBLANES
    final = hf is not None

    def tix(i):
        return (nt - 1 - i) if rev else i

    in_specs = [pl.BlockSpec((None, tb, U_BLK), lambda b, i: (b, tix(i), U_BX)),
                pl.BlockSpec((None, SUBLANES, U_BLK),
                             lambda b, i: (b, jnp.maximum(tix(i) * r8 - 1, 0), U_BX)),
                pl.BlockSpec((None, SUBLANES, U_BLK),
                             lambda b, i: (b, jnp.minimum((tix(i) + 1) * r8, T // SUBLANES - 1), U_BX)),
                pl.BlockSpec((RG_CONV, W), lambda b, i: (0, 0)),
                pl.BlockSpec((1, W), lambda b, i: (0, 0)),
                pl.BlockSpec((RG_BLOCKS, RG_BLOCK, 2 * RG_BLOCK), lambda b, i: (0, 0, 0)),
                pl.BlockSpec((2, W), lambda b, i: (0, 0)),
                pl.BlockSpec((1, W), lambda b, i: (0, 0)),
                pl.BlockSpec((None, 1, W), lambda b, i: (b, 0, 0))]
    args = [u, u, u, conv_w, conv_b.reshape(1, W), wri, bri, lam.reshape(1, W), h0]
    if final:
        in_specs += [pl.BlockSpec((None, tb, U_BLK), lambda b, i: (b, tix(i), U_BG)),
                     pl.BlockSpec((None, tb, W), lambda b, i: (b, tix(i), 0))]
        args += [u, hf]
    return pl.pallas_call(
        functools.partial(_rglru_kernel, rev=rev, tb=tb, nt=nt, final=final), grid=(B, nt),
        in_specs=in_specs,
        out_specs=pl.BlockSpec((None, tb, W), lambda b, i: (b, tix(i), 0)),
        out_shape=jax.ShapeDtypeStruct((B, T, W), BF16 if final else F32),
        scratch_shapes=[pltpu.VMEM((tb + 2 * SUBLANES, W), F32), pltpu.VMEM((1, W), F32)],
        compiler_params=_params(2), name="rglru_bwd" if rev else "rglru_fwd")(*args)


HGRN_CHUNK = 128
HGRN_SUB = 8


def _hgrn_chunk(qv, z, v, lb, st, ones, rev):
    C = HGRN_CHUNK
    sg = _sigmoid(z)
    k = (1.0 - lb) * _sigmoid(-z)
    g = jnp.log(lb + (1.0 - lb) * sg)
    row = lax.broadcasted_iota(jnp.int32, (C, LANES), 0)
    col = lax.broadcasted_iota(jnp.int32, (C, C), 1)
    rowc = lax.broadcasted_iota(jnp.int32, (C, C), 0)

    G = g
    d = 1
    while d < C:
        if rev:
            G = G + jnp.where(row < C - d, pltpu.roll(G, C - d, 0), 0.0)
        else:
            G = G + jnp.where(row >= d, pltpu.roll(G, d, 0), 0.0)
        d *= 2
    G = G * LOG2E
    last = 0 if rev else C - 1
    g_last = G[last:last + 1, :]

    qe = (qv * jnp.exp2(G)).astype(BF16)
    o = _dot_nt(qe, st.astype(BF16))
    kdec = (k * jnp.exp2(g_last - G)).astype(BF16)
    vb = v.astype(BF16)
    st_new = st * jnp.exp2(g_last) + _dot(v.T.astype(BF16), kdec)

    pos = (C - 1 - row) if rev else row
    a_sum = jnp.zeros((C, C), F32)
    hsz = HGRN_SUB
    while hsz < C:
        grp = 2 * hsz
        ng = C // grp
        G3 = G.reshape(ng, grp, LANES)
        ref = G3[:, hsz:hsz + 1, :] if rev else G3[:, hsz - 1:hsz, :]
        ref = jnp.broadcast_to(ref, (ng, grp, LANES)).reshape(C, LANES)
        e = jnp.exp2(-jnp.abs(G - ref))
        later = ((pos >> int(np.log2(hsz))) & 1) == 1
        ql = jnp.where(later, qv * e, 0.0).astype(BF16)
        kl = jnp.where(later, 0.0, k * e).astype(BF16)
        al = _dot_nt(ql, kl)
        shift = int(np.log2(grp))
        a_sum = a_sum + jnp.where((rowc >> shift) == (col >> shift), al, 0.0)
        hsz = grp
    o = o + _dot(a_sum.astype(BF16), vb)

    nb = C // HGRN_SUB
    q3 = qv.reshape(nb, HGRN_SUB, LANES)
    k3 = k.reshape(nb, HGRN_SUB, LANES)
    v3 = v.reshape(nb, HGRN_SUB, LANES)
    G3 = G.reshape(nb, HGRN_SUB, LANES)
    rb = lax.broadcasted_iota(jnp.int32, (nb, HGRN_SUB, LANES), 1)
    od = jnp.zeros((nb, HGRN_SUB, LANES), F32)
    for s in range(HGRN_SUB):
        valid = (rb <= s) if rev else (rb >= s)
        e = jnp.exp2(G3 - G3[:, s:s + 1, :])
        p = jnp.where(valid, q3 * k3[:, s:s + 1, :] * e, 0.0)
        rs = _dot(p.reshape(C, LANES).astype(BF16), ones)
        od = od + rs.reshape(nb, HGRN_SUB, LANES) * v3[:, s:s + 1, :]
    return o + od.reshape(C, LANES), st_new


def _hgrn_kernel(*refs, rev, tb, nt, final):
    if final:
        (q_ref, z_ref, v_ref, lb_ref, s0_ref, ones_ref, of_ref, og_ref, gn_ref,
         o_ref, sfin_ref, st_ref) = refs
    else:
        q_ref, z_ref, v_ref, lb_ref, s0_ref, ones_ref, o_ref, sfin_ref, st_ref = refs
    i = pl.program_id(2)

    @pl.when(i == 0)
    def _():
        st_ref[...] = s0_ref[...]

    lb = lb_ref[...]
    ones = ones_ref[...]
    C = HGRN_CHUNK
    nch = tb // C

    def body(c, carry):
        cc = (nch - 1 - c) if rev else c
        r0 = pl.multiple_of(cc * C, C)
        qin = q_ref[pl.ds(r0, C), :]
        qv = qin * _sigmoid(qin)
        o, st_new = _hgrn_chunk(qv, z_ref[pl.ds(r0, C), :], v_ref[pl.ds(r0, C), :], lb,
                                st_ref[...], ones, rev)
        st_ref[...] = st_new
        if final:
            o = o + of_ref[pl.ds(r0, C), :]
            og = og_ref[pl.ds(r0, C), :]
            o = _rms(o, gn_ref[...]) * (og * _sigmoid(og))
        o_ref[pl.ds(r0, C), :] = o.astype(o_ref.dtype)
        return carry

    lax.fori_loop(0, nch, body, 0, unroll=True)

    @pl.when(i == nt - 1)
    def _():
        sfin_ref[...] = st_ref[...]


def _hgrn_pass(u, z_blk, lb, s0, ones, of, g_norm, rev):
    B, T, _ = u.shape
    H = HGRN_HEADS
    tb = _pick(T, 512)
    nt = T // tb
    final = of is not None

    def tix(i):
        return (nt - 1 - i) if rev else i

    def ublk(blk):
        return pl.BlockSpec((None, tb, LANES), lambda b, h, i: (b, tix(i), blk * (U_BLK // LANES) + h))

    in_specs = [ublk(U_DQ), ublk(z_blk), ublk(U_DI),
                pl.BlockSpec((1, LANES), lambda b, h, i: (0, h)),
                pl.BlockSpec((None, None, HGRN_V, HGRN_K), lambda b, h, i: (b, h, 0, 0)),
                pl.BlockSpec((LANES, LANES), lambda b, h, i: (0, 0))]
    args = [u, u, u, lb.reshape(1, -1), s0, ones]
    if final:
        in_specs += [pl.BlockSpec((None, tb, LANES), lambda b, h, i: (b, tix(i), h)),
                     ublk(U_DG),
                     pl.BlockSpec((1, HGRN_V), lambda b, h, i: (0, 0))]
        args += [of, u, g_norm.reshape(1, HGRN_V)]
    return pl.pallas_call(
        functools.partial(_hgrn_kernel, rev=rev, tb=tb, nt=nt, final=final), grid=(B, H, nt),
        in_specs=in_specs,
        out_specs=[pl.BlockSpec((None, tb, LANES), lambda b, h, i: (b, tix(i), h)),
                   pl.BlockSpec((None, None, HGRN_V, HGRN_K), lambda b, h, i: (b, h, 0, 0))],
        out_shape=[jax.ShapeDtypeStruct((B, T, H * HGRN_V), BF16 if final else F32),
                   jax.ShapeDtypeStruct((B, H, HGRN_V, HGRN_K), F32)],
        scratch_shapes=[pltpu.VMEM((HGRN_V, HGRN_K), F32)],
        compiler_params=_params(3), name="hgrn_bwd" if rev else "hgrn_fwd")(*args)


def _merge_kernel(x_ref, mod_ref, pre_ref, post_ref, ya_ref, yb_ref, yc_ref, yd_ref, w_ref, wb_ref, mb_ref,
                  o_ref, h_ref, acc_ref):
    n = pl.program_id(2)

    @pl.when(n == 0)
    def _():
        h_ref[...] = _prenorm(x_ref[...], mod_ref, 1, pre_ref[...]).astype(BF16)
        acc_ref[...] = jnp.zeros(acc_ref.shape, F32)

    for nb, y_ref in enumerate((ya_ref, yb_ref, yc_ref, yd_ref)):
        @pl.when(n == nb)
        def _(y_ref=y_ref):
            gate = _sigmoid(_dot(h_ref[...], w_ref[...]) + mb_ref[...])
            acc_ref[...] += gate * _dot(y_ref[...], wb_ref[...])

    @pl.when(n == N_BRANCH)
    def _():
        mix = _dot(acc_ref[...].astype(BF16), w_ref[...])
        o_ref[...] = x_ref[...] + mod_ref[5:6, :] * _rms(mix, post_ref[...])


def _merge(x, mod, pre_g, post_g, ys, w5, wb, mb, l):
    B, T, D = x.shape
    tm = _pick(T, 512)
    last = N_BRANCH - 1
    yspec = pl.BlockSpec((None, tm, BRANCH_W), lambda b, t, n: (b, t, 0))
    return pl.pallas_call(
        _merge_kernel, grid=(B, T // tm, N_BRANCH + 1),
        in_specs=[pl.BlockSpec((None, tm, D), lambda b, t, n: (b, t, 0)),
                  pl.BlockSpec((None, N_MOD, D), lambda b, t, n: (b, 0, 0)),
                  pl.BlockSpec((1, D), lambda b, t, n: (0, 0)),
                  pl.BlockSpec((1, D), lambda b, t, n: (0, 0)),
                  yspec, yspec, yspec, yspec,
                  pl.BlockSpec((D, D), lambda b, t, n: (0, n)),
                  pl.BlockSpec((None, None, BRANCH_W, D), lambda b, t, n: (l, jnp.minimum(n, last), 0, 0)),
                  pl.BlockSpec((None, None, 1, D), lambda b, t, n: (l, jnp.minimum(n, last), 0, 0))],
        out_specs=pl.BlockSpec((None, tm, D), lambda b, t, n: (b, t, 0)),
        out_shape=jax.ShapeDtypeStruct(x.shape, F32),
        scratch_shapes=[pltpu.VMEM((tm, D), BF16), pltpu.VMEM((tm, D), F32)],
        compiler_params=_params(3), name="merge")(
            x, mod, pre_g.reshape(1, D), post_g.reshape(1, D), *ys, w5, wb, mb)


def _rope_tables(n_tok):
    axis_dim = MLA_ROPE // 2
    inv_freq = ROPE_BASE ** (-jnp.arange(0, axis_dim, 2, dtype=F32) / axis_dim)
    tok = lax.broadcasted_iota(jnp.int32, (n_tok, LANES), 0)
    lane = lax.broadcasted_iota(jnp.int32, (n_tok, LANES), 1)
    pos = jnp.where((lane & 32) == 0, tok // GRID_W, tok % GRID_W).astype(F32)
    ang = pos * jnp.tile(inv_freq, LANES // inv_freq.shape[0])[None, :]
    cos = jnp.cos(ang)
    sin = jnp.where((lane & 16) == 0, -jnp.sin(ang), jnp.sin(ang))
    first = lane < MLA_ROPE
    return cos, sin, jnp.where(first, cos, 1.0), jnp.where(first, sin, 0.0)


def _layer_weights(l, w_in, mla_w_uq, mla_w_ukv, rg_w_r, rg_w_i, rg_b_r, rg_b_i, w_out):
    D = D_MODEL
    cuts = np.cumsum([MLA_Q_LORA, MLA_KV_LORA, MLA_ROPE, RG_WIDTH, RG_WIDTH, 512, 512, 512,
                      512, 512, 512, 512, 512])
    wi = w_in[l]
    parts = jnp.split(wi[:, :cuts[-1]], cuts[:-1].tolist(), axis=1)
    (a_q, a_kv, a_kr, b_x, b_g, c_q, c_k, c_v, d_q, d_ff, d_fb, d_i, d_g) = parts
    pad = jnp.zeros((D, U_BLK - MLA_KV_LORA - MLA_ROPE), wi.dtype)
    w_u = jnp.concatenate([a_q, b_x, b_g, c_q, c_k, c_v, d_q, d_ff, d_fb, d_i, d_g, a_kv, a_kr, pad],
                          axis=1).astype(BF16)
    w5 = jnp.concatenate([wi[:, cuts[-1]:], w_out[l]], axis=1).astype(BF16)

    wq = mla_w_uq[l].reshape(MLA_Q_LORA, MLA_HEADS, MLA_NOPE + MLA_ROPE)
    wq = jnp.pad(wq, ((0, 0), (0, 0), (0, 256 - MLA_NOPE - MLA_ROPE))).reshape(MLA_Q_LORA, -1).astype(BF16)
    wkv = mla_w_ukv[l].reshape(MLA_KV_LORA, MLA_HEADS, MLA_NOPE + MLA_V)
    wk = wkv[:, :, :MLA_NOPE].reshape(MLA_KV_LORA, -1).astype(BF16)
    wvt = wkv[:, :, MLA_NOPE:].reshape(MLA_KV_LORA, -1).T.astype(BF16)

    wri = [jnp.concatenate([rg_w_r[l, d], rg_w_i[l, d]], axis=-1).astype(BF16) for d in range(2)]
    bri = [jnp.stack([rg_b_r[l, d], rg_b_i[l, d]], axis=0) for d in range(2)]
    return w_u, w5, wq, wk, wvt, wri, bri


def kernel(x, c, ctx, c_ctx, ada_w, ada_b, pre_norm, post_norm, ffn_w_gate, ffn_w_up, ffn_w_down, w_in, mla_q_norm, mla_w_uq, mla_kv_norm, mla_w_ukv, rg_conv_w, rg_conv_b, rg_w_r, rg_b_r, rg_w_i, rg_b_i, rg_lambda, diff_lambda, diff_subln, hgrn_lb_logits, hgrn_norm, merge_b, w_branch, w_out):
    B, S, D = x.shape
    Tc = ctx.shape[1]
    depth = ada_w.shape[0]
    tkc_mla = _pick(S, MLA_TKC)
    tkc_diff = _pick(S, DIFF_TKC)

    rc_diff, rs_diff, rc_mla, rs_mla = _rope_tables(S)
    rc_ctx = jnp.ones((Tc, LANES), F32)
    rs_ctx = jnp.zeros((Tc, LANES), F32)

    lb_soft = jax.nn.softmax(hgrn_lb_logits.astype(F32), axis=0)
    hgrn_lb = jnp.cumsum(lb_soft, axis=0) - lb_soft[0]

    c8 = jnp.zeros((SUBLANES, D), F32).at[:B].set(c).at[B].set(c_ctx)
    eye = jnp.eye(DIFF_HEADS * DIFF_V, dtype=BF16)
    ones = jnp.ones((LANES, LANES), BF16)
    zero_state = jnp.zeros((B, HGRN_HEADS, HGRN_V, HGRN_K), F32)
    zero_h = jnp.zeros((B, 1, RG_WIDTH), F32)

    wg = ffn_w_gate.astype(BF16)
    wu = ffn_w_up.astype(BF16)
    wd = ffn_w_down.astype(BF16)
    wb = w_branch.astype(BF16)
    mb = merge_b.reshape(depth, N_BRANCH, 1, D)

    xc = ctx
    for l in range(depth):
        need_ctx = l < depth - 1
        lam_init = 0.8 - 0.6 * float(np.exp(-0.3 * l))
        mod8 = _ada_mod(c8, ada_w, ada_b, l).reshape(SUBLANES, N_MOD, D)
        mod = mod8[:B]
        mod_c = jnp.broadcast_to(mod8[B:B + 1], (B, N_MOD, D))
        w_u, w5, wq, wk, wvt, wri, bri = _layer_weights(
            l, w_in, mla_w_uq, mla_w_ukv, rg_w_r, rg_w_i, rg_b_r, rg_b_i, w_out)

        x = _ffn(x, mod, 0, pre_norm[l, 0], post_norm[l, 0], wg, wu, wd, l, 0)
        xc = _ffn(xc, mod_c, 0, pre_norm[l, 0], post_norm[l, 0], wg, wu, wd, l, 0)

        u = _inproj(x, mod, pre_norm[l, 1], w_u)
        uc = _inproj(xc, mod_c, pre_norm[l, 1], w_u)

        qa_c, ka_c, vta_c = _mla_prep(uc, rc_ctx, rs_ctx, mla_q_norm[l], mla_kv_norm[l], wq, wk, wvt, Tc, False)
        qa, ka, vta = _mla_prep(u, rc_mla, rs_mla, mla_q_norm[l], mla_kv_norm[l], wq, wk, wvt, tkc_mla, True)
        y_a = _mla_attn(qa, ka_c, vta_c, ka, vta)

        qd_c, kd_c, vtd_c = _diff_prep(uc, rc_ctx, rs_ctx, eye, Tc, False)
        qd, kd, vtd = _diff_prep(u, rc_diff, rs_diff, eye, tkc_diff, True)
        y_c = _diff_attn(qd, kd_c, vtd_c, kd, vtd, diff_lambda[l], diff_subln[l], lam_init)

        hc_f = _rglru_pass(uc, rg_conv_w[l], rg_conv_b[l], wri[0], bri[0], rg_lambda[l, 0], zero_h, None, False)
        hl_f = _rglru_pass(u, rg_conv_w[l], rg_conv_b[l], wri[0], bri[0], rg_lambda[l, 0],
                           hc_f[:, Tc - 1:Tc, :], None, False)
        if need_ctx:
            y_b_c = _rglru_pass(uc, rg_conv_w[l], rg_conv_b[l], wri[1], bri[1], rg_lambda[l, 1], zero_h, hc_f, True)
        hc_b = _rglru_pass(uc, rg_conv_w[l], rg_conv_b[l], wri[1], bri[1], rg_lambda[l, 1], zero_h, None, True)
        y_b = _rglru_pass(u, rg_conv_w[l], rg_conv_b[l], wri[1], bri[1], rg_lambda[l, 1],
                          hc_b[:, 0:1, :], hl_f, True)

        oc_f, s_f = _hgrn_pass(uc, U_DFF, hgrn_lb[l], zero_state, ones, None, None, False)
        if need_ctx:
            y_d_c, s_b = _hgrn_pass(uc, U_DFB, hgrn_lb[l], zero_state, ones, oc_f, hgrn_norm[l], True)
        else:
            _, s_b = _hgrn_pass(uc, U_DFB, hgrn_lb[l], zero_state, ones, None, None, True)
        o_f, _ = _hgrn_pass(u, U_DFF, hgrn_lb[l], s_f, ones, None, None, False)
        y_d, _ = _hgrn_pass(u, U_DFB, hgrn_lb[l], s_b, ones, o_f, hgrn_norm[l], True)

        x = _merge(x, mod, pre_norm[l, 1], post_norm[l, 1], (y_a, y_b, y_c, y_d), w5, wb, mb, l)
        if need_ctx:
            y_a_c = _mla_attn(qa_c, ka_c, vta_c, None, None)
            y_c_c = _diff_attn(qd_c, kd_c, vtd_c, None, None, diff_lambda[l], diff_subln[l], lam_init)
            xc = _merge(xc, mod_c, pre_norm[l, 1], post_norm[l, 1], (y_a_c, y_b_c, y_c_c, y_d_c), w5, wb, mb, l)
            xc = _ffn(xc, mod_c, 2, pre_norm[l, 2], post_norm[l, 2], wg, wu, wd, l, 1)

        x = _ffn(x, mod, 2, pre_norm[l, 2], post_norm[l, 2], wg, wu, wd, l, 1)
    return x
```

```python
import functools
from typing import Any, NamedTuple

import jax
import jax.numpy as jnp
import numpy as np
from jax import lax
from jax.experimental import pallas as pl
from jax.experimental.pallas import tpu as pltpu

F32 = jnp.float32
BF16 = jnp.bfloat16

D_MODEL = 2048
GRID_W = 64
N_SUB = 3
N_MOD = 3 * N_SUB
FFN_RES = 0.5
NORM_EPS = 1e-6
ROPE_BASE = 10000.0

MLA_HEADS = 4
MLA_Q_LORA = 512
MLA_KV_LORA = 256
MLA_NOPE = 128
MLA_ROPE = 64
MLA_V = 128
MLA_SCALE = (MLA_NOPE + MLA_ROPE) ** -0.5

RG_WIDTH = 512
RG_BLOCKS = 4
RG_BLOCK = RG_WIDTH // RG_BLOCKS
RG_CONV = 4
RG_C = 8.0

DIFF_HEADS = 4
DIFF_QK = 64
DIFF_V = 2 * DIFF_QK
DIFF_SCALE = DIFF_QK ** -0.5

HGRN_HEADS = 4
HGRN_K = 128
HGRN_V = 128

N_BRANCH = 4
BRANCH_W = 512

LANES = 128
SUBLANES = 8
VMEM_LIMIT_BYTES = 60 * 1024 * 1024
FFN_TM = 1024
FFN_TF = 256

NT_DIMS = (((1,), (1,)), ((), ()))


def _params(n_axes):
    return pltpu.CompilerParams(dimension_semantics=("arbitrary",) * n_axes,
                                vmem_limit_bytes=VMEM_LIMIT_BYTES)


def _pick(n, pref):
    t = min(n, pref)
    assert n % t == 0, (n, pref)
    return t


def _rms(x, g):
    return x * lax.rsqrt(jnp.mean(x * x, axis=-1, keepdims=True) + NORM_EPS) * g


def _sigmoid(x):
    return jax.nn.sigmoid(x)


def _dot(a, b):
    return jnp.dot(a, b, preferred_element_type=F32)


def _dot_nt(a, b):
    return lax.dot_general(a, b, NT_DIMS, preferred_element_type=F32)


def _ada_kernel(c_ref, w_ref, b_ref, o_ref):
    c = c_ref[...]
    cond = (c * _sigmoid(c)).astype(BF16)
    o_ref[...] = _dot(cond, w_ref[...].astype(BF16)) + b_ref[...]


def _ada_mod(c8, w, b, l):
    L, D, N = w.shape
    tn = _pick(N, 2048)
    return pl.pallas_call(
        _ada_kernel, grid=(N // tn,),
        in_specs=[pl.BlockSpec((SUBLANES, D), lambda n: (0, 0)),
                  pl.BlockSpec((None, D, tn), lambda n: (l, 0, n)),
                  pl.BlockSpec((None, 1, tn), lambda n: (l, 0, n))],
        out_specs=pl.BlockSpec((SUBLANES, tn), lambda n: (0, n)),
        out_shape=jax.ShapeDtypeStruct((SUBLANES, N), F32),
        compiler_params=_params(1), name="ada_mod")(c8, w, b.reshape(L, 1, N))


def _prenorm(x, mod_ref, j, g):
    return _rms(x, g) * (1.0 + mod_ref[3 * j + 1:3 * j + 2, :]) + mod_ref[3 * j:3 * j + 1, :]


def _ffn_kernel(x_ref, mod_ref, pre_ref, post_ref, wg_ref, wu_ref, wd_ref, o_ref, h_ref, *, j, nf):
    f = pl.program_id(2)

    @pl.when(f == 0)
    def _():
        h_ref[...] = _prenorm(x_ref[...], mod_ref, j, pre_ref[...]).astype(BF16)
        o_ref[...] = jnp.zeros(o_ref.shape, F32)

    h = h_ref[...]
    g = _dot(h, wg_ref[...])
    u = _dot(h, wu_ref[...])
    a = (g * _sigmoid(g) * u).astype(BF16)
    o_ref[...] += _dot(a, wd_ref[...])

    @pl.when(f == nf - 1)
    def _():
        yn = _rms(o_ref[...], post_ref[...])
        o_ref[...] = x_ref[...] + FFN_RES * mod_ref[3 * j + 2:3 * j + 3, :] * yn


def _ffn(x, mod, j, pre_g, post_g, wg, wu, wd, l, i):
    B, T, D = x.shape
    F = wg.shape[-1]
    tm = _pick(T, FFN_TM)
    tf = _pick(F, FFN_TF)
    nf = F // tf
    return pl.pallas_call(
        functools.partial(_ffn_kernel, j=j, nf=nf), grid=(B, T // tm, nf),
        in_specs=[pl.BlockSpec((None, tm, D), lambda b, t, f: (b, t, 0)),
                  pl.BlockSpec((None, N_MOD, D), lambda b, t, f: (b, 0, 0)),
                  pl.BlockSpec((1, D), lambda b, t, f: (0, 0)),
                  pl.BlockSpec((1, D), lambda b, t, f: (0, 0)),
                  pl.BlockSpec((None, None, D, tf), lambda b, t, f: (l, i, 0, f)),
                  pl.BlockSpec((None, None, D, tf), lambda b, t, f: (l, i, 0, f)),
                  pl.BlockSpec((None, None, tf, D), lambda b, t, f: (l, i, f, 0))],
        out_specs=pl.BlockSpec((None, tm, D), lambda b, t, f: (b, t, 0)),
        out_shape=jax.ShapeDtypeStruct(x.shape, F32),
        scratch_shapes=[pltpu.VMEM((tm, D), BF16)],
        compiler_params=_params(3), name="ffn")(
            x, mod, pre_g.reshape(1, D), post_g.reshape(1, D), wg, wu, wd)


def _inproj_kernel(x_ref, mod_ref, pre_ref, w_ref, o_ref, h_ref):
    @pl.when(pl.program_id(2) == 0)
    def _():
        h_ref[...] = _prenorm(x_ref[...], mod_ref, 1, pre_ref[...]).astype(BF16)

    o_ref[...] = _dot(h_ref[...], w_ref[...])


def _inproj(x, mod, pre_g, w):
    B, T, D = x.shape
    N = w.shape[1]
    tm = _pick(T, 1024)
    tn = _pick(N, 512)
    return pl.pallas_call(
        _inproj_kernel, grid=(B, T // tm, N // tn),
        in_specs=[pl.BlockSpec((None, tm, D), lambda b, t, n: (b, t, 0)),
                  pl.BlockSpec((None, N_MOD, D), lambda b, t, n: (b, 0, 0)),
                  pl.BlockSpec((1, D), lambda b, t, n: (0, 0)),
                  pl.BlockSpec((D, tn), lambda b, t, n: (0, n))],
        out_specs=pl.BlockSpec((None, tm, tn), lambda b, t, n: (b, t, n)),
        out_shape=jax.ShapeDtypeStruct((B, T, N), F32),
        scratch_shapes=[pltpu.VMEM((tm, D), BF16)],
        compiler_params=_params(3), name="inproj")(x, mod, pre_g.reshape(1, D), w)


U_AQ, U_BX, U_BG, U_CQ, U_CK, U_CV, U_DQ, U_DFF, U_DFB, U_DI, U_DG, U_AKV = range(12)
U_BLK = 512


def _rope128(x, c, s):
    lane = lax.broadcasted_iota(jnp.int32, x.shape, 1)
    partner = jnp.where((lane & 31) < 16, pltpu.roll(x, LANES - 16, 1), pltpu.roll(x, 16, 1))
    return x * c + partner * s


def _mla_prep_kernel(uq_ref, ukv_ref, c_ref, s_ref, qn_ref, kvn_ref, wq_ref, wk_ref, wvt_ref,
                     q_ref, k_ref, vt_ref, *, rope):
    uqn = _rms(uq_ref[...], qn_ref[...]).astype(BF16)
    q = _dot(uqn, wq_ref[...]) * (MLA_SCALE * LOG2E)
    ukv = ukv_ref[...]
    ukvn = _rms(ukv[:, :MLA_KV_LORA], kvn_ref[...]).astype(BF16)
    kn = _dot(ukvn, wk_ref[...])
    kr = ukv[:, MLA_KV_LORA:MLA_KV_LORA + LANES]
    if rope:
        c = c_ref[...]
        s = s_ref[...]
        kr = _rope128(kr, c, s)
    krb = kr.astype(BF16)
    for h in range(MLA_HEADS):
        qr = q[:, 256 * h + 128:256 * h + 256]
        if rope:
            qr = _rope128(qr, c, s)
        q_ref[:, 256 * h:256 * h + 128] = q[:, 256 * h:256 * h + 128].astype(BF16)
        q_ref[:, 256 * h + 128:256 * h + 256] = qr.astype(BF16)
        k_ref[:, 256 * h:256 * h + 128] = kn[:, 128 * h:128 * h + 128].astype(BF16)
        k_ref[:, 256 * h + 128:256 * h + 256] = krb
    vt = _dot_nt(wvt_ref[...], ukvn)
    tm = vt.shape[1]
    vt_ref[...] = vt.reshape(MLA_HEADS, MLA_V, tm).astype(BF16)


def _mla_prep(u, rc, rs, q_norm, kv_norm, wq, wk, wvt, tkc, rope):
    B, T, _ = u.shape
    tm = tkc
    nC = T // tm
    H = MLA_HEADS
    return pl.pallas_call(
        functools.partial(_mla_prep_kernel, rope=rope), grid=(B, nC),
        in_specs=[pl.BlockSpec((None, tm, U_BLK), lambda b, t: (b, t, U_AQ)),
                  pl.BlockSpec((None, tm, U_BLK), lambda b, t: (b, t, U_AKV)),
                  pl.BlockSpec((tm, LANES), lambda b, t: (t, 0)),
                  pl.BlockSpec((tm, LANES), lambda b, t: (t, 0)),
                  pl.BlockSpec((1, MLA_Q_LORA), lambda b, t: (0, 0)),
                  pl.BlockSpec((1, MLA_KV_LORA), lambda b, t: (0, 0)),
                  pl.BlockSpec(wq.shape, lambda b, t: (0, 0)),
                  pl.BlockSpec(wk.shape, lambda b, t: (0, 0)),
                  pl.BlockSpec(wvt.shape, lambda b, t: (0, 0))],
        out_specs=[pl.BlockSpec((None, tm, H * 256), lambda b, t: (b, t, 0)),
                   pl.BlockSpec((None, tm, H * 256), lambda b, t: (b, t, 0)),
                   pl.BlockSpec((None, H, None, MLA_V, tm), lambda b, t: (b, 0, t, 0, 0))],
        out_shape=[jax.ShapeDtypeStruct((B, T, H * 256), BF16),
                   jax.ShapeDtypeStruct((B, T, H * 256), BF16),
                   jax.ShapeDtypeStruct((B, H, nC, MLA_V, tm), BF16)],
        compiler_params=_params(2), name="mla_prep")(
            u, u, rc, rs, q_norm.reshape(1, -1), kv_norm.reshape(1, -1), wq, wk, wvt)


ATTN_SUB = 512
MLA_TQ = 1024
DIFF_TQ = 512
MLA_TKC = 1024
DIFF_TKC = 1024
LOG2E = float(np.log2(np.e))


class _Stream(NamedTuple):
    m: Any
    l: Any
    acc: Any
    s: Any = None
    cm: Any = None
    p: Any = None
    al: Any = None


def _stream_scratch(tq, tkc, dv):
    shapes = [pltpu.VMEM((1, tq), F32), pltpu.VMEM((1, tq), F32), pltpu.VMEM((dv, tq), F32)]
    if tkc:
        shapes += [pltpu.VMEM((2, tkc, tq), F32), pltpu.VMEM((2, 1, tq), F32),
                   pltpu.VMEM((2, tkc, tq), BF16), pltpu.VMEM((2, 1, tq), F32)]
    return shapes


def _load_block(ref, r0, nr, c0, nc):
    return ref[r0:r0 + nr, c0:c0 + nc]


def _softmax_step(k, vt, q, st):
    s = _dot_nt(k(), q())
    m_prev = st.m[...]
    m_new = jnp.maximum(m_prev, jnp.max(s, axis=0, keepdims=True))
    alpha = jnp.exp2(m_prev - m_new)
    p = jnp.exp2(s - m_new)
    st.l[...] = alpha * st.l[...] + jnp.sum(p, axis=0, keepdims=True)
    st.acc[...] = alpha * st.acc[...] + _dot(vt(), p.astype(BF16))
    st.m[...] = m_new


def _stage_scores(k, qs, streams, slot):
    for q, st in zip(qs, streams):
        s = _dot_nt(k(), q())
        st.s[slot] = s
        st.cm[slot] = jnp.max(s, axis=0, keepdims=True)


def _stage_softmax(streams, slot):
    for st in streams:
        m_prev = st.m[...]
        m_new = jnp.maximum(m_prev, st.cm[slot])
        alpha = jnp.exp2(m_prev - m_new)
        p = jnp.exp2(st.s[slot] - m_new)
        st.l[...] = alpha * st.l[...] + jnp.sum(p, axis=0, keepdims=True)
        st.p[slot] = p.astype(BF16)
        st.al[slot] = alpha
        st.m[...] = m_new


def _stage_values(vt, streams, slot):
    for st in streams:
        st.acc[...] = st.al[slot] * st.acc[...] + _dot(vt(), st.p[slot])


def _attend(qs, kc_ref, vtc_ref, kl_ref, vtl_ref, n_lat, tkc, streams):
    for st in streams:
        st.m[...] = jnp.full(st.m.shape, -jnp.inf, F32)
        st.l[...] = jnp.zeros(st.l.shape, F32)
        st.acc[...] = jnp.zeros(st.acc.shape, F32)
    for q, st in zip(qs, streams):
        _softmax_step(lambda: kc_ref[...], lambda: vtc_ref[...], q, st)
    if not n_lat:
        return

    def keys(j):
        return lambda: kl_ref[pl.ds(pl.multiple_of(j * tkc, tkc), tkc), :]

    def vals(j):
        return lambda: vtl_ref[j]

    if n_lat < 4 or n_lat % 2:
        def plain(j, carry):
            for q, st in zip(qs, streams):
                _softmax_step(keys(j), vals(j), q, st)
            return carry
        lax.fori_loop(0, n_lat, plain, 0)
        return

    _stage_scores(keys(0), qs, streams, 0)
    _stage_scores(keys(1), qs, streams, 1)
    _stage_softmax(streams, 0)

    def body(i, carry):
        j = 2 * i
        _stage_scores(keys(j), qs, streams, 0)
        _stage_softmax(streams, 1)
        _stage_values(vals(j - 2), streams, 0)
        _stage_scores(keys(j + 1), qs, streams, 1)
        _stage_softmax(streams, 0)
        _stage_values(vals(j - 1), streams, 1)
        return carry

    lax.fori_loop(1, n_lat // 2, body, 0)
    _stage_softmax(streams, 1)
    _stage_values(vals(n_lat - 2), streams, 0)
    _stage_values(vals(n_lat - 1), streams, 1)


def _split_streams(scratch, n):
    per = len(scratch) // n
    return [_Stream(*scratch[i * per:(i + 1) * per]) for i in range(n)]


def _mla_attn_kernel(*refs, n_lat, tkc, n_sub, ts):
    if n_lat:
        q_ref, kc_ref, vtc_ref, kl_ref, vtl_ref, o_ref = refs[:6]
        scratch = refs[6:]
    else:
        q_ref, kc_ref, vtc_ref, o_ref = refs[:4]
        kl_ref = vtl_ref = None
        scratch = refs[4:]
    streams = _split_streams(scratch, n_sub)
    qs = [functools.partial(_load_block, q_ref, sub * ts, ts, 0, q_ref.shape[1]) for sub in range(n_sub)]
    _attend(qs, kc_ref, vtc_ref, kl_ref, vtl_ref, n_lat, tkc, streams)
    for sub, st in enumerate(streams):
        o = st.acc[...] / st.l[...]
        o_ref[sub * ts:(sub + 1) * ts, :] = o.T.astype(o_ref.dtype)


def _mla_attn(q, k_ctx, vt_ctx, k_lat, vt_lat):
    B, Tq, _ = q.shape
    H = MLA_HEADS
    Tc = k_ctx.shape[1]
    tq = _pick(Tq, MLA_TQ)
    n_lat = 0 if k_lat is None else vt_lat.shape[2]
    tkc = 0 if k_lat is None else vt_lat.shape[4]
    in_specs = [pl.BlockSpec((None, tq, 256), lambda b, h, i: (b, i, h)),
                pl.BlockSpec((None, Tc, 256), lambda b, h, i: (b, 0, h)),
                pl.BlockSpec((None, None, None, MLA_V, Tc), lambda b, h, i: (b, h, 0, 0, 0))]
    args = [q, k_ctx, vt_ctx]
    if n_lat:
        T = k_lat.shape[1]
        in_specs += [pl.BlockSpec((None, T, 256), lambda b, h, i: (b, 0, h)),
                     pl.BlockSpec((None, None, n_lat, MLA_V, tkc), lambda b, h, i: (b, h, 0, 0, 0))]
        args += [k_lat, vt_lat]
    ts = _pick(tq, ATTN_SUB)
    n_sub = tq // ts
    return pl.pallas_call(
        functools.partial(_mla_attn_kernel, n_lat=n_lat, tkc=tkc, n_sub=n_sub, ts=ts),
        grid=(B, H, Tq // tq), in_specs=in_specs,
        out_specs=pl.BlockSpec((None, tq, MLA_V), lambda b, h, i: (b, i, h)),
        out_shape=jax.ShapeDtypeStruct((B, Tq, H * MLA_V), BF16),
        scratch_shapes=_stream_scratch(ts, tkc, MLA_V) * n_sub,
        compiler_params=_params(3), name="mla_attn")(*args)


def _diff_prep_kernel(uq_ref, uk_ref, uv_ref, c_ref, s_ref, eye_ref, q_ref, k_ref, vt_ref, *, rope):
    uq = uq_ref[...] * (DIFF_SCALE * LOG2E)
    uk = uk_ref[...]
    lane = lax.broadcasted_iota(jnp.int32, (uq.shape[0], LANES), 1)
    if rope:
        c = c_ref[...]
        s = s_ref[...]
    for h in range(DIFF_HEADS):
        qh = uq[:, LANES * h:LANES * (h + 1)]
        kh = uk[:, LANES * h:LANES * (h + 1)]
        if rope:
            qh = _rope128(qh, c, s)
            kh = _rope128(kh, c, s)
        q_ref[:, 256 * h:256 * h + 128] = jnp.where(lane < DIFF_QK, qh, 0.0).astype(BF16)
        q_ref[:, 256 * h + 128:256 * h + 256] = jnp.where(lane < DIFF_QK, 0.0, qh).astype(BF16)
        k_ref[:, LANES * h:LANES * (h + 1)] = kh.astype(BF16)
    vt = _dot_nt(eye_ref[...], uv_ref[...].astype(BF16))
    tm = vt.shape[1]
    vt_ref[...] = vt.reshape(DIFF_HEADS, DIFF_V, tm).astype(BF16)


def _diff_prep(u, rc, rs, eye, tkc, rope):
    B, T, _ = u.shape
    tm = tkc
    nC = T // tm
    H = DIFF_HEADS
    return pl.pallas_call(
        functools.partial(_diff_prep_kernel, rope=rope), grid=(B, nC),
        in_specs=[pl.BlockSpec((None, tm, U_BLK), lambda b, t: (b, t, U_CQ)),
                  pl.BlockSpec((None, tm, U_BLK), lambda b, t: (b, t, U_CK)),
                  pl.BlockSpec((None, tm, U_BLK), lambda b, t: (b, t, U_CV)),
                  pl.BlockSpec((tm, LANES), lambda b, t: (t, 0)),
                  pl.BlockSpec((tm, LANES), lambda b, t: (t, 0)),
                  pl.BlockSpec(eye.shape, lambda b, t: (0, 0))],
        out_specs=[pl.BlockSpec((None, tm, H * 256), lambda b, t: (b, t, 0)),
                   pl.BlockSpec((None, tm, H * LANES), lambda b, t: (b, t, 0)),
                   pl.BlockSpec((None, H, None, DIFF_V, tm), lambda b, t: (b, 0, t, 0, 0))],
        out_shape=[jax.ShapeDtypeStruct((B, T, H * 256), BF16),
                   jax.ShapeDtypeStruct((B, T, H * LANES), BF16),
                   jax.ShapeDtypeStruct((B, H, nC, DIFF_V, tm), BF16)],
        compiler_params=_params(2), name="diff_prep")(u, u, u, rc, rs, eye)


def _diff_attn_kernel(*refs, n_lat, tkc, lam_init, n_sub, ts):
    if n_lat:
        q_ref, kc_ref, vtc_ref, kl_ref, vtl_ref, lv_ref, sub_ref, o_ref = refs[:8]
        scratch = refs[8:]
    else:
        q_ref, kc_ref, vtc_ref, lv_ref, sub_ref, o_ref = refs[:6]
        kl_ref = vtl_ref = None
        scratch = refs[6:]
    streams = _split_streams(scratch, 2 * n_sub)
    qs = []
    for sub in range(n_sub):
        qs += [functools.partial(_load_block, q_ref, sub * ts, ts, 0, LANES),
               functools.partial(_load_block, q_ref, sub * ts, ts, LANES, LANES)]
    _attend(qs, kc_ref, vtc_ref, kl_ref, vtl_ref, n_lat, tkc, streams)
    lv = lv_ref[...]
    lam = (jnp.exp(jnp.sum(lv[0:1] * lv[1:2], axis=-1, keepdims=True))
           - jnp.exp(jnp.sum(lv[2:3] * lv[3:4], axis=-1, keepdims=True)) + lam_init)
    for sub in range(n_sub):
        st0, st1 = streams[2 * sub], streams[2 * sub + 1]
        o = st0.acc[...] / st0.l[...] - lam * (st1.acc[...] / st1.l[...])
        ot = o.T
        o_ref[sub * ts:(sub + 1) * ts, :] = (_rms(ot, sub_ref[...]) * (1.0 - lam_init)).astype(o_ref.dtype)


def _diff_attn(q, k_ctx, vt_ctx, k_lat, vt_lat, lam_vecs, subln, lam_init):
    B, Tq, _ = q.shape
    H = DIFF_HEADS
    Tc = k_ctx.shape[1]
    tq = _pick(Tq, DIFF_TQ)
    n_lat = 0 if k_lat is None else vt_lat.shape[2]
    tkc = 0 if k_lat is None else vt_lat.shape[4]
    in_specs = [pl.BlockSpec((None, tq, 256), lambda b, h, i: (b, i, h)),
                pl.BlockSpec((None, Tc, LANES), lambda b, h, i: (b, 0, h)),
                pl.BlockSpec((None, None, None, DIFF_V, Tc), lambda b, h, i: (b, h, 0, 0, 0))]
    args = [q, k_ctx, vt_ctx]
    if n_lat:
        T = k_lat.shape[1]
        in_specs += [pl.BlockSpec((None, T, LANES), lambda b, h, i: (b, 0, h)),
                     pl.BlockSpec((None, None, n_lat, DIFF_V, tkc), lambda b, h, i: (b, h, 0, 0, 0))]
        args += [k_lat, vt_lat]
    in_specs += [pl.BlockSpec((4, DIFF_QK), lambda b, h, i: (0, 0)),
                 pl.BlockSpec((1, DIFF_V), lambda b, h, i: (0, 0))]
    args += [lam_vecs, subln.reshape(1, DIFF_V)]
    ts = _pick(tq, ATTN_SUB)
    n_sub = tq // ts
    scratch = _stream_scratch(ts, tkc, DIFF_V) * (2 * n_sub)
    return pl.pallas_call(
        functools.partial(_diff_attn_kernel, n_lat=n_lat, tkc=tkc, lam_init=lam_init, n_sub=n_sub, ts=ts),
        grid=(B, H, Tq // tq), in_specs=in_specs,
        out_specs=pl.BlockSpec((None, tq, DIFF_V), lambda b, h, i: (b, i, h)),
        out_shape=jax.ShapeDtypeStruct((B, Tq, H * DIFF_V), BF16),
        scratch_shapes=scratch, compiler_params=_params(3), name="diff_attn")(*args)


def _gelu_tanh(x):
    return 0.5 * x * (1.0 + jnp.tanh(np.sqrt(2.0 / np.pi) * (x + 0.044715 * (x * x * x))))


def _rglru_kernel(*refs, rev, tb, nt, final):
    if final:
        (x_ref, xp_ref, xn_ref, cw_ref, cb_ref, wri_ref, bri_ref, lam_ref, h0_ref, gate_ref, hf_ref,
         o_ref, ext_ref, carry_ref) = refs
    else:
        (x_ref, xp_ref, xn_ref, cw_ref, cb_ref, wri_ref, bri_ref, lam_ref, h0_ref,
         o_ref, ext_ref, carry_ref) = refs
    i = pl.program_id(1)
    blk = (nt - 1 - i) if rev else i

    @pl.when(i == 0)
    def _():
        carry_ref[...] = h0_ref[...]

    ext_ref[0:SUBLANES, :] = jnp.where(blk > 0, xp_ref[...], 0.0)
    ext_ref[SUBLANES:SUBLANES + tb, :] = x_ref[...]
    ext_ref[SUBLANES + tb:2 * SUBLANES + tb, :] = jnp.where(blk < nt - 1, xn_ref[...], 0.0)
    left = RG_CONV // 2
    xc = cb_ref[...]
    for tap in range(RG_CONV):
        off = SUBLANES - left + tap
        xc = xc + cw_ref[tap:tap + 1, :] * ext_ref[off:off + tb, :]
    xcb = xc.astype(BF16)

    rs, gs = [], []
    for kb in range(RG_BLOCKS):
        ri = _dot(xcb[:, RG_BLOCK * kb:RG_BLOCK * (kb + 1)], wri_ref[kb])
        rs.append(ri[:, :RG_BLOCK])
        gs.append(ri[:, RG_BLOCK:])
    bri = bri_ref[...]
    r = _sigmoid(jnp.concatenate(rs, axis=1) + bri[0:1])
    ig = _sigmoid(jnp.concatenate(gs, axis=1) + bri[1:2])
    lam = lam_ref[...]
    softplus_neg = jnp.maximum(-lam, 0.0) + jnp.log(1.0 + jnp.exp(-jnp.abs(lam)))
    log_a = -RG_C * r * softplus_neg
    a = jnp.exp(log_a)
    bv = jnp.sqrt(-jnp.tanh(log_a) * (a * a + 1.0)) * (ig * xc)

    row = lax.broadcasted_iota(jnp.int32, a.shape, 0)
    d = 1
    while d < tb:
        if rev:
            keep = row < tb - d
            a_sh = pltpu.roll(a, tb - d, 0)
            b_sh = pltpu.roll(bv, tb - d, 0)
        else:
            keep = row >= d
            a_sh = pltpu.roll(a, d, 0)
            b_sh = pltpu.roll(bv, d, 0)
        bv = a * jnp.where(keep, b_sh, 0.0) + bv
        a = a * jnp.where(keep, a_sh, 1.0)
        d *= 2
    h = a * carry_ref[...] + bv
    last = 0 if rev else tb - 1
    carry_ref[...] = h[last:last + 1, :]
    if final:
        o_ref[...] = ((hf_ref[...] + h) * _gelu_tanh(gate_ref[...])).astype(o_ref.dtype)
    else:
        o_ref[...] = h


def _rglru_pass(u, conv_w, conv_b, wri, bri, lam, h0, hf, rev):
    B, T, _ = u.shape
    W = RG_WIDTH
    tb = _pick(T, 256)
    nt = T // tb
    r8 = tb // SUBLANES
    final = hf is not None

    def tix(i):
        return (nt - 1 - i) if rev else i

    in_specs = [pl.BlockSpec((None, tb, U_BLK), lambda b, i: (b, tix(i), U_BX)),
                pl.BlockSpec((None, SUBLANES, U_BLK),
                             lambda b, i: (b, jnp.maximum(tix(i) * r8 - 1, 0), U_BX)),
                pl.BlockSpec((None, SUBLANES, U_BLK),
                             lambda b, i: (b, jnp.minimum((tix(i) + 1) * r8, T // SUBLANES - 1), U_BX)),
                pl.BlockSpec((RG_CONV, W), lambda b, i: (0, 0)),
                pl.BlockSpec((1, W), lambda b, i: (0, 0)),
                pl.BlockSpec((RG_BLOCKS, RG_BLOCK, 2 * RG_BLOCK), lambda b, i: (0, 0, 0)),
                pl.BlockSpec((2, W), lambda b, i: (0, 0)),
                pl.BlockSpec((1, W), lambda b, i: (0, 0)),
                pl.BlockSpec((None, 1, W), lambda b, i: (b, 0, 0))]
    args = [u, u, u, conv_w, conv_b.reshape(1, W), wri, bri, lam.reshape(1, W), h0]
    if final:
        in_specs += [pl.BlockSpec((None, tb, U_BLK), lambda b, i: (b, tix(i), U_BG)),
                     pl.BlockSpec((None, tb, W), lambda b, i: (b, tix(i), 0))]
        args += [u, hf]
    return pl.pallas_call(
        functools.partial(_rglru_kernel, rev=rev, tb=tb, nt=nt, final=final), grid=(B, nt),
        in_specs=in_specs,
        out_specs=pl.BlockSpec((None, tb, W), lambda b, i: (b, tix(i), 0)),
        out_shape=jax.ShapeDtypeStruct((B, T, W), BF16 if final else F32),
        scratch_shapes=[pltpu.VMEM((tb + 2 * SUBLANES, W), F32), pltpu.VMEM((1, W), F32)],
        compiler_params=_params(2), name="rglru_bwd" if rev else "rglru_fwd")(*args)


HGRN_CHUNK = 128
HGRN_SUB = 8


def _hgrn_chunk(qv, z, v, lb, st, ones, rev):
    C = HGRN_CHUNK
    sg = _sigmoid(z)
    k = (1.0 - lb) * _sigmoid(-z)
    g = jnp.log(lb + (1.0 - lb) * sg)
    row = lax.broadcasted_iota(jnp.int32, (C, LANES), 0)
    col = lax.broadcasted_iota(jnp.int32, (C, C), 1)
    rowc = lax.broadcasted_iota(jnp.int32, (C, C), 0)

    G = g
    d = 1
    while d < C:
        if rev:
            G = G + jnp.where(row < C - d, pltpu.roll(G, C - d, 0), 0.0)
        else:
            G = G + jnp.where(row >= d, pltpu.roll(G, d, 0), 0.0)
        d *= 2
    G = G * LOG2E
    last = 0 if rev else C - 1
    g_last = G[last:last + 1, :]

    qe = (qv * jnp.exp2(G)).astype(BF16)
    o = _dot_nt(qe, st.astype(BF16))
    kdec = (k * jnp.exp2(g_last - G)).astype(BF16)
    vb = v.astype(BF16)
    st_new = st * jnp.exp2(g_last) + _dot(v.T.astype(BF16), kdec)

    pos = (C - 1 - row) if rev else row
    a_sum = jnp.zeros((C, C), F32)
    hsz = HGRN_SUB
    while hsz < C:
        grp = 2 * hsz
        ng = C // grp
        G3 = G.reshape(ng, grp, LANES)
        ref = G3[:, hsz:hsz + 1, :] if rev else G3[:, hsz - 1:hsz, :]
        ref = jnp.broadcast_to(ref, (ng, grp, LANES)).reshape(C, LANES)
        e = jnp.exp2(-jnp.abs(G - ref))
        later = ((pos >> int(np.log2(hsz))) & 1) == 1
        ql = jnp.where(later, qv * e, 0.0).astype(BF16)
        kl = jnp.where(later, 0.0, k * e).astype(BF16)
        al = _dot_nt(ql, kl)
        shift = int(np.log2(grp))
        a_sum = a_sum + jnp.where((rowc >> shift) == (col >> shift), al, 0.0)
        hsz = grp
    o = o + _dot(a_sum.astype(BF16), vb)

    nb = C // HGRN_SUB
    q3 = qv.reshape(nb, HGRN_SUB, LANES)
    k3 = k.reshape(nb, HGRN_SUB, LANES)
    v3 = v.reshape(nb, HGRN_SUB, LANES)
    G3 = G.reshape(nb, HGRN_SUB, LANES)
    rb = lax.broadcasted_iota(jnp.int32, (nb, HGRN_SUB, LANES), 1)
    od = jnp.zeros((nb, HGRN_SUB, LANES), F32)
    for s in range(HGRN_SUB):
        valid = (rb <= s) if rev else (rb >= s)
        e = jnp.exp2(G3 - G3[:, s:s + 1, :])
        p = jnp.where(valid, q3 * k3[:, s:s + 1, :] * e, 0.0)
        rs = _dot(p.reshape(C, LANES).astype(BF16), ones)
        od = od + rs.reshape(nb, HGRN_SUB, LANES) * v3[:, s:s + 1, :]
    return o + od.reshape(C, LANES), st_new


def _hgrn_kernel(*refs, rev, tb, nt, final):
    if final:
        (q_ref, z_ref, v_ref, lb_ref, s0_ref, ones_ref, of_ref, og_ref, gn_ref,
         o_ref, sfin_ref, st_ref) = refs
    else:
        q_ref, z_ref, v_ref, lb_ref, s0_ref, ones_ref, o_ref, sfin_ref, st_ref = refs
    i = pl.program_id(2)

    @pl.when(i == 0)
    def _():
        st_ref[...] = s0_ref[...]

    lb = lb_ref[...]
    ones = ones_ref[...]
    C = HGRN_CHUNK
    nch = tb // C

    def body(c, carry):
        cc = (nch - 1 - c) if rev else c
        r0 = pl.multiple_of(cc * C, C)
        qin = q_ref[pl.ds(r0, C), :]
        qv = qin * _sigmoid(qin)
        o, st_new = _hgrn_chunk(qv, z_ref[pl.ds(r0, C), :], v_ref[pl.ds(r0, C), :], lb,
                                st_ref[...], ones, rev)
        st_ref[...] = st_new
        if final:
            o = o + of_ref[pl.ds(r0, C), :]
            og = og_ref[pl.ds(r0, C), :]
            o = _rms(o, gn_ref[...]) * (og * _sigmoid(og))
        o_ref[pl.ds(r0, C), :] = o.astype(o_ref.dtype)
        return carry

    lax.fori_loop(0, nch, body, 0, unroll=True)

    @pl.when(i == nt - 1)
    def _():
        sfin_ref[...] = st_ref[...]


def _hgrn_pass(u, z_blk, lb, s0, ones, of, g_norm, rev):
    B, T, _ = u.shape
    H = HGRN_HEADS
    tb = _pick(T, 512)
    nt = T // tb
    final = of is not None

    def tix(i):
        return (nt - 1 - i) if rev else i

    def ublk(blk):
        return pl.BlockSpec((None, tb, LANES), lambda b, h, i: (b, tix(i), blk * (U_BLK // LANES) + h))

    in_specs = [ublk(U_DQ), ublk(z_blk), ublk(U_DI),
                pl.BlockSpec((1, LANES), lambda b, h, i: (0, h)),
                pl.BlockSpec((None, None, HGRN_V, HGRN_K), lambda b, h, i: (b, h, 0, 0)),
                pl.BlockSpec((LANES, LANES), lambda b, h, i: (0, 0))]
    args = [u, u, u, lb.reshape(1, -1), s0, ones]
    if final:
        in_specs += [pl.BlockSpec((None, tb, LANES), lambda b, h, i: (b, tix(i), h)),
                     ublk(U_DG),
                     pl.BlockSpec((1, HGRN_V), lambda b, h, i: (0, 0))]
        args += [of, u, g_norm.reshape(1, HGRN_V)]
    return pl.pallas_call(
        functools.partial(_hgrn_kernel, rev=rev, tb=tb, nt=nt, final=final), grid=(B, H, nt),
        in_specs=in_specs,
        out_specs=[pl.BlockSpec((None, tb, LANES), lambda b, h, i: (b, tix(i), h)),
                   pl.BlockSpec((None, None, HGRN_V, HGRN_K), lambda b, h, i: (b, h, 0, 0))],
        out_shape=[jax.ShapeDtypeStruct((B, T, H * HGRN_V), BF16 if final else F32),
                   jax.ShapeDtypeStruct((B, H, HGRN_V, HGRN_K), F32)],
        scratch_shapes=[pltpu.VMEM((HGRN_V, HGRN_K), F32)],
        compiler_params=_params(3), name="hgrn_bwd" if rev else "hgrn_fwd")(*args)


def _merge_kernel(x_ref, mod_ref, pre_ref, post_ref, ya_ref, yb_ref, yc_ref, yd_ref, w_ref, wb_ref, mb_ref,
                  o_ref, h_ref, acc_ref):
    n = pl.program_id(2)

    @pl.when(n == 0)
    def _():
        h_ref[...] = _prenorm(x_ref[...], mod_ref, 1, pre_ref[...]).astype(BF16)
        acc_ref[...] = jnp.zeros(acc_ref.shape, F32)

    for nb, y_ref in enumerate((ya_ref, yb_ref, yc_ref, yd_ref)):
        @pl.when(n == nb)
        def _(y_ref=y_ref):
            gate = _sigmoid(_dot(h_ref[...], w_ref[...]) + mb_ref[...])
            acc_ref[...] += gate * _dot(y_ref[...], wb_ref[...])

    @pl.when(n == N_BRANCH)
    def _():
        mix = _dot(acc_ref[...].astype(BF16), w_ref[...])
        o_ref[...] = x_ref[...] + mod_ref[5:6, :] * _rms(mix, post_ref[...])


def _merge(x, mod, pre_g, post_g, ys, w5, wb, mb, l):
    B, T, D = x.shape
    tm = _pick(T, 512)
    last = N_BRANCH - 1
    yspec = pl.BlockSpec((None, tm, BRANCH_W), lambda b, t, n: (b, t, 0))
    return pl.pallas_call(
        _merge_kernel, grid=(B, T // tm, N_BRANCH + 1),
        in_specs=[pl.BlockSpec((None, tm, D), lambda b, t, n: (b, t, 0)),
                  pl.BlockSpec((None, N_MOD, D), lambda b, t, n: (b, 0, 0)),
                  pl.BlockSpec((1, D), lambda b, t, n: (0, 0)),
                  pl.BlockSpec((1, D), lambda b, t, n: (0, 0)),
                  yspec, yspec, yspec, yspec,
                  pl.BlockSpec((D, D), lambda b, t, n: (0, n)),
                  pl.BlockSpec((None, None, BRANCH_W, D), lambda b, t, n: (l, jnp.minimum(n, last), 0, 0)),
                  pl.BlockSpec((None, None, 1, D), lambda b, t, n: (l, jnp.minimum(n, last), 0, 0))],
        out_specs=pl.BlockSpec((None, tm, D), lambda b, t, n: (b, t, 0)),
        out_shape=jax.ShapeDtypeStruct(x.shape, F32),
        scratch_shapes=[pltpu.VMEM((tm, D), BF16), pltpu.VMEM((tm, D), F32)],
        compiler_params=_params(3), name="merge")(
            x, mod, pre_g.reshape(1, D), post_g.reshape(1, D), *ys, w5, wb, mb)


def _rope_tables(n_tok):
    axis_dim = MLA_ROPE // 2
    inv_freq = ROPE_BASE ** (-jnp.arange(0, axis_dim, 2, dtype=F32) / axis_dim)
    tok = lax.broadcasted_iota(jnp.int32, (n_tok, LANES), 0)
    lane = lax.broadcasted_iota(jnp.int32, (n_tok, LANES), 1)
    pos = jnp.where((lane & 32) == 0, tok // GRID_W, tok % GRID_W).astype(F32)
    ang = pos * jnp.tile(inv_freq, LANES // inv_freq.shape[0])[None, :]
    cos = jnp.cos(ang)
    sin = jnp.where((lane & 16) == 0, -jnp.sin(ang), jnp.sin(ang))
    first = lane < MLA_ROPE
    return cos, sin, jnp.where(first, cos, 1.0), jnp.where(first, sin, 0.0)


def _layer_weights(l, w_in, mla_w_uq, mla_w_ukv, rg_w_r, rg_w_i, rg_b_r, rg_b_i, w_out):
    D = D_MODEL
    cuts = np.cumsum([MLA_Q_LORA, MLA_KV_LORA, MLA_ROPE, RG_WIDTH, RG_WIDTH, 512, 512, 512,
                      512, 512, 512, 512, 512])
    wi = w_in[l]
    parts = jnp.split(wi[:, :cuts[-1]], cuts[:-1].tolist(), axis=1)
    (a_q, a_kv, a_kr, b_x, b_g, c_q, c_k, c_v, d_q, d_ff, d_fb, d_i, d_g) = parts
    pad = jnp.zeros((D, U_BLK - MLA_KV_LORA - MLA_ROPE), wi.dtype)
    w_u = jnp.concatenate([a_q, b_x, b_g, c_q, c_k, c_v, d_q, d_ff, d_fb, d_i, d_g, a_kv, a_kr, pad],
                          axis=1).astype(BF16)
    w5 = jnp.concatenate([wi[:, cuts[-1]:], w_out[l]], axis=1).astype(BF16)

    wq = mla_w_uq[l].reshape(MLA_Q_LORA, MLA_HEADS, MLA_NOPE + MLA_ROPE)
    wq = jnp.pad(wq, ((0, 0), (0, 0), (0, 256 - MLA_NOPE - MLA_ROPE))).reshape(MLA_Q_LORA, -1).astype(BF16)
    wkv = mla_w_ukv[l].reshape(MLA_KV_LORA, MLA_HEADS, MLA_NOPE + MLA_V)
    wk = wkv[:, :, :MLA_NOPE].reshape(MLA_KV_LORA, -1).astype(BF16)
    wvt = wkv[:, :, MLA_NOPE:].reshape(MLA_KV_LORA, -1).T.astype(BF16)

    wri = [jnp.concatenate([rg_w_r[l, d], rg_w_i[l, d]], axis=-1).astype(BF16) for d in range(2)]
    bri = [jnp.stack([rg_b_r[l, d], rg_b_i[l, d]], axis=0) for d in range(2)]
    return w_u, w5, wq, wk, wvt, wri, bri


def kernel(x, c, ctx, c_ctx, ada_w, ada_b, pre_norm, post_norm, ffn_w_gate, ffn_w_up, ffn_w_down, w_in, mla_q_norm, mla_w_uq, mla_kv_norm, mla_w_ukv, rg_conv_w, rg_conv_b, rg_w_r, rg_b_r, rg_w_i, rg_b_i, rg_lambda, diff_lambda, diff_subln, hgrn_lb_logits, hgrn_norm, merge_b, w_branch, w_out):
    B, S, D = x.shape
    Tc = ctx.shape[1]
    depth = ada_w.shape[0]
    tkc_mla = _pick(S, MLA_TKC)
    tkc_diff = _pick(S, DIFF_TKC)

    rc_diff, rs_diff, rc_mla, rs_mla = _rope_tables(S)
    rc_ctx = jnp.ones((Tc, LANES), F32)
    rs_ctx = jnp.zeros((Tc, LANES), F32)

    lb_soft = jax.nn.softmax(hgrn_lb_logits.astype(F32), axis=0)
    hgrn_lb = jnp.cumsum(lb_soft, axis=0) - lb_soft[0]

    c8 = jnp.zeros((SUBLANES, D), F32).at[:B].set(c).at[B].set(c_ctx)
    eye = jnp.eye(DIFF_HEADS * DIFF_V, dtype=BF16)
    ones = jnp.ones((LANES, LANES), BF16)
    zero_state = jnp.zeros((B, HGRN_HEADS, HGRN_V, HGRN_K), F32)
    zero_h = jnp.zeros((B, 1, RG_WIDTH), F32)

    wg = ffn_w_gate.astype(BF16)
    wu = ffn_w_up.astype(BF16)
    wd = ffn_w_down.astype(BF16)
    wb = w_branch.astype(BF16)
    mb = merge_b.reshape(depth, N_BRANCH, 1, D)

    xc = ctx
    for l in range(depth):
        need_ctx = l < depth - 1
        lam_init = 0.8 - 0.6 * float(np.exp(-0.3 * l))
        mod8 = _ada_mod(c8, ada_w, ada_b, l).reshape(SUBLANES, N_MOD, D)
        mod = mod8[:B]
        mod_c = jnp.broadcast_to(mod8[B:B + 1], (B, N_MOD, D))
        w_u, w5, wq, wk, wvt, wri, bri = _layer_weights(
            l, w_in, mla_w_uq, mla_w_ukv, rg_w_r, rg_w_i, rg_b_r, rg_b_i, w_out)

        x = _ffn(x, mod, 0, pre_norm[l, 0], post_norm[l, 0], wg, wu, wd, l, 0)
        xc = _ffn(xc, mod_c, 0, pre_norm[l, 0], post_norm[l, 0], wg, wu, wd, l, 0)

        u = _inproj(x, mod, pre_norm[l, 1], w_u)
        uc = _inproj(xc, mod_c, pre_norm[l, 1], w_u)

        qa_c, ka_c, vta_c = _mla_prep(uc, rc_ctx, rs_ctx, mla_q_norm[l], mla_kv_norm[l], wq, wk, wvt, Tc, False)
        qa, ka, vta = _mla_prep(u, rc_mla, rs_mla, mla_q_norm[l], mla_kv_norm[l], wq, wk, wvt, tkc_mla, True)
        y_a = _mla_attn(qa, ka_c, vta_c, ka, vta)

        qd_c, kd_c, vtd_c = _diff_prep(uc, rc_ctx, rs_ctx, eye, Tc, False)
        qd, kd, vtd = _diff_prep(u, rc_diff, rs_diff, eye, tkc_diff, True)
        y_c = _diff_attn(qd, kd_c, vtd_c, kd, vtd, diff_lambda[l], diff_subln[l], lam_init)

        hc_f = _rglru_pass(uc, rg_conv_w[l], rg_conv_b[l], wri[0], bri[0], rg_lambda[l, 0], zero_h, None, False)
        hl_f = _rglru_pass(u, rg_conv_w[l], rg_conv_b[l], wri[0], bri[0], rg_lambda[l, 0],
                           hc_f[:, Tc - 1:Tc, :], None, False)
        if need_ctx:
            y_b_c = _rglru_pass(uc, rg_conv_w[l], rg_conv_b[l], wri[1], bri[1], rg_lambda[l, 1], zero_h, hc_f, True)
        hc_b = _rglru_pass(uc, rg_conv_w[l], rg_conv_b[l], wri[1], bri[1], rg_lambda[l, 1], zero_h, None, True)
        y_b = _rglru_pass(u, rg_conv_w[l], rg_conv_b[l], wri[1], bri[1], rg_lambda[l, 1],
                          hc_b[:, 0:1, :], hl_f, True)

        oc_f, s_f = _hgrn_pass(uc, U_DFF, hgrn_lb[l], zero_state, ones, None, None, False)
        if need_ctx:
            y_d_c, s_b = _hgrn_pass(uc, U_DFB, hgrn_lb[l], zero_state, ones, oc_f, hgrn_norm[l], True)
        else:
            _, s_b = _hgrn_pass(uc, U_DFB, hgrn_lb[l], zero_state, ones, None, None, True)
        o_f, _ = _hgrn_pass(u, U_DFF, hgrn_lb[l], s_f, ones, None, None, False)
        y_d, _ = _hgrn_pass(u, U_DFB, hgrn_lb[l], s_b, ones, o_f, hgrn_norm[l], True)

        x = _merge(x, mod, pre_norm[l, 1], post_norm[l, 1], (y_a, y_b, y_c, y_d), w5, wb, mb, l)
        if need_ctx:
            y_a_c = _mla_attn(qa_c, ka_c, vta_c, None, None)
            y_c_c = _diff_attn(qd_c, kd_c, vtd_c, None, None, diff_lambda[l], diff_subln[l], lam_init)
            xc = _merge(xc, mod_c, pre_norm[l, 1], post_norm[l, 1], (y_a_c, y_b_c, y_c_c, y_d_c), w5, wb, mb, l)
            xc = _ffn(xc, mod_c, 2, pre_norm[l, 2], post_norm[l, 2], wg, wu, wd, l, 1)

        x = _ffn(x, mod, 2, pre_norm[l, 2], post_norm[l, 2], wg, wu, wd, l, 1)
    return x
```

```python
import functools
from typing import Any, NamedTuple

import jax
import jax.numpy as jnp
import numpy as np
from jax import lax
from jax.experimental import pallas as pl
from jax.experimental.pallas import tpu as pltpu

F32 = jnp.float32
BF16 = jnp.bfloat16

D_MODEL = 2048
GRID_W = 64
N_SUB = 3
N_MOD = 3 * N_SUB
FFN_RES = 0.5
NORM_EPS = 1e-6
ROPE_BASE = 10000.0

MLA_HEADS = 4
MLA_Q_LORA = 512
MLA_KV_LORA = 256
MLA_NOPE = 128
MLA_ROPE = 64
MLA_V = 128
MLA_SCALE = (MLA_NOPE + MLA_ROPE) ** -0.5

RG_WIDTH = 512
RG_BLOCKS = 4
RG_BLOCK = RG_WIDTH // RG_BLOCKS
RG_CONV = 4
RG_C = 8.0

DIFF_HEADS = 4
DIFF_QK = 64
DIFF_V = 2 * DIFF_QK
DIFF_SCALE = DIFF_QK ** -0.5

HGRN_HEADS = 4
HGRN_K = 128
HGRN_V = 128

N_BRANCH = 4
BRANCH_W = 512

LANES = 128
SUBLANES = 8
VMEM_LIMIT_BYTES = 60 * 1024 * 1024
FFN_TM = 1024
FFN_TF = 256

NT_DIMS = (((1,), (1,)), ((), ()))


def _params(n_axes):
    return pltpu.CompilerParams(dimension_semantics=("arbitrary",) * n_axes,
                                vmem_limit_bytes=VMEM_LIMIT_BYTES)


def _pick(n, pref):
    t = min(n, pref)
    assert n % t == 0, (n, pref)
    return t


def _rms(x, g):
    return x * lax.rsqrt(jnp.mean(x * x, axis=-1, keepdims=True) + NORM_EPS) * g


def _sigmoid(x):
    return jax.nn.sigmoid(x)


def _dot(a, b):
    return jnp.dot(a, b, preferred_element_type=F32)


def _dot_nt(a, b):
    return lax.dot_general(a, b, NT_DIMS, preferred_element_type=F32)


def _ada_kernel(c_ref, w_ref, b_ref, o_ref):
    c = c_ref[...]
    cond = (c * _sigmoid(c)).astype(BF16)
    o_ref[...] = _dot(cond, w_ref[...].astype(BF16)) + b_ref[...]


def _ada_mod(c8, w, b, l):
    L, D, N = w.shape
    tn = _pick(N, 2048)
    return pl.pallas_call(
        _ada_kernel, grid=(N // tn,),
        in_specs=[pl.BlockSpec((SUBLANES, D), lambda n: (0, 0)),
                  pl.BlockSpec((None, D, tn), lambda n: (l, 0, n)),
                  pl.BlockSpec((None, 1, tn), lambda n: (l, 0, n))],
        out_specs=pl.BlockSpec((SUBLANES, tn), lambda n: (0, n)),
        out_shape=jax.ShapeDtypeStruct((SUBLANES, N), F32),
        compiler_params=_params(1), name="ada_mod")(c8, w, b.reshape(L, 1, N))


def _prenorm(x, mod_ref, j, g):
    return _rms(x, g) * (1.0 + mod_ref[3 * j + 1:3 * j + 2, :]) + mod_ref[3 * j:3 * j + 1, :]


def _ffn_kernel(x_ref, mod_ref, pre_ref, post_ref, wg_ref, wu_ref, wd_ref, o_ref, h_ref, *, j, nf):
    f = pl.program_id(2)

    @pl.when(f == 0)
    def _():
        h_ref[...] = _prenorm(x_ref[...], mod_ref, j, pre_ref[...]).astype(BF16)
        o_ref[...] = jnp.zeros(o_ref.shape, F32)

    h = h_ref[...]
    g = _dot(h, wg_ref[...])
    u = _dot(h, wu_ref[...])
    a = (g * _sigmoid(g) * u).astype(BF16)
    o_ref[...] += _dot(a, wd_ref[...])

    @pl.when(f == nf - 1)
    def _():
        yn = _rms(o_ref[...], post_ref[...])
        o_ref[...] = x_ref[...] + FFN_RES * mod_ref[3 * j + 2:3 * j + 3, :] * yn


def _ffn(x, mod, j, pre_g, post_g, wg, wu, wd, l, i):
    B, T, D = x.shape
    F = wg.shape[-1]
    tm = _pick(T, FFN_TM)
    tf = _pick(F, FFN_TF)
    nf = F // tf
    return pl.pallas_call(
        functools.partial(_ffn_kernel, j=j, nf=nf), grid=(B, T // tm, nf),
        in_specs=[pl.BlockSpec((None, tm, D), lambda b, t, f: (b, t, 0)),
                  pl.BlockSpec((None, N_MOD, D), lambda b, t, f: (b, 0, 0)),
                  pl.BlockSpec((1, D), lambda b, t, f: (0, 0)),
                  pl.BlockSpec((1, D), lambda b, t, f: (0, 0)),
                  pl.BlockSpec((None, None, D, tf), lambda b, t, f: (l, i, 0, f)),
                  pl.BlockSpec((None, None, D, tf), lambda b, t, f: (l, i, 0, f)),
                  pl.BlockSpec((None, None, tf, D), lambda b, t, f: (l, i, f, 0))],
        out_specs=pl.BlockSpec((None, tm, D), lambda b, t, f: (b, t, 0)),
        out_shape=jax.ShapeDtypeStruct(x.shape, F32),
        scratch_shapes=[pltpu.VMEM((tm, D), BF16)],
        compiler_params=_params(3), name="ffn")(
            x, mod, pre_g.reshape(1, D), post_g.reshape(1, D), wg, wu, wd)


def _inproj_kernel(x_ref, mod_ref, pre_ref, w_ref, o_ref, h_ref):
    @pl.when(pl.program_id(2) == 0)
    def _():
        h_ref[...] = _prenorm(x_ref[...], mod_ref, 1, pre_ref[...]).astype(BF16)

    o_ref[...] = _dot(h_ref[...], w_ref[...])


def _inproj(x, mod, pre_g, w):
    B, T, D = x.shape
    N = w.shape[1]
    tm = _pick(T, 1024)
    tn = _pick(N, 512)
    return pl.pallas_call(
        _inproj_kernel, grid=(B, T // tm, N // tn),
        in_specs=[pl.BlockSpec((None, tm, D), lambda b, t, n: (b, t, 0)),
                  pl.BlockSpec((None, N_MOD, D), lambda b, t, n: (b, 0, 0)),
                  pl.BlockSpec((1, D), lambda b, t, n: (0, 0)),
                  pl.BlockSpec((D, tn), lambda b, t, n: (0, n))],
        out_specs=pl.BlockSpec((None, tm, tn), lambda b, t, n: (b, t, n)),
        out_shape=jax.ShapeDtypeStruct((B, T, N), F32),
        scratch_shapes=[pltpu.VMEM((tm, D), BF16)],
        compiler_params=_params(3), name="inproj")(x, mod, pre_g.reshape(1, D), w)


U_AQ, U_BX, U_BG, U_CQ, U_CK, U_CV, U_DQ, U_DFF, U_DFB, U_DI, U_DG, U_AKV = range(12)
U_BLK = 512


def _rope128(x, c, s):
    lane = lax.broadcasted_iota(jnp.int32, x.shape, 1)
    partner = jnp.where((lane & 31) < 16, pltpu.roll(x, LANES - 16, 1), pltpu.roll(x, 16, 1))
    return x * c + partner * s


def _mla_prep_kernel(uq_ref, ukv_ref, c_ref, s_ref, qn_ref, kvn_ref, wq_ref, wk_ref, wvt_ref,
                     q_ref, k_ref, vt_ref, *, rope):
    uqn = _rms(uq_ref[...], qn_ref[...]).astype(BF16)
    q = _dot(uqn, wq_ref[...]) * (MLA_SCALE * LOG2E)
    ukv = ukv_ref[...]
    ukvn = _rms(ukv[:, :MLA_KV_LORA], kvn_ref[...]).astype(BF16)
    kn = _dot(ukvn, wk_ref[...])
    kr = ukv[:, MLA_KV_LORA:MLA_KV_LORA + LANES]
    if rope:
        c = c_ref[...]
        s = s_ref[...]
        kr = _rope128(kr, c, s)
    krb = kr.astype(BF16)
    for h in range(MLA_HEADS):
        qr = q[:, 256 * h + 128:256 * h + 256]
        if rope:
            qr = _rope128(qr, c, s)
        q_ref[:, 256 * h:256 * h + 128] = q[:, 256 * h:256 * h + 128].astype(BF16)
        q_ref[:, 256 * h + 128:256 * h + 256] = qr.astype(BF16)
        k_ref[:, 256 * h:256 * h + 128] = kn[:, 128 * h:128 * h + 128].astype(BF16)
        k_ref[:, 256 * h + 128:256 * h + 256] = krb
    vt = _dot_nt(wvt_ref[...], ukvn)
    tm = vt.shape[1]
    vt_ref[...] = vt.reshape(MLA_HEADS, MLA_V, tm).astype(BF16)


def _mla_prep(u, rc, rs, q_norm, kv_norm, wq, wk, wvt, tkc, rope):
    B, T, _ = u.shape
    tm = tkc
    nC = T // tm
    H = MLA_HEADS
    return pl.pallas_call(
        functools.partial(_mla_prep_kernel, rope=rope), grid=(B, nC),
        in_specs=[pl.BlockSpec((None, tm, U_BLK), lambda b, t: (b, t, U_AQ)),
                  pl.BlockSpec((None, tm, U_BLK), lambda b, t: (b, t, U_AKV)),
                  pl.BlockSpec((tm, LANES), lambda b, t: (t, 0)),
                  pl.BlockSpec((tm, LANES), lambda b, t: (t, 0)),
                  pl.BlockSpec((1, MLA_Q_LORA), lambda b, t: (0, 0)),
                  pl.BlockSpec((1, MLA_KV_LORA), lambda b, t: (0, 0)),
                  pl.BlockSpec(wq.shape, lambda b, t: (0, 0)),
                  pl.BlockSpec(wk.shape, lambda b, t: (0, 0)),
                  pl.BlockSpec(wvt.shape, lambda b, t: (0, 0))],
        out_specs=[pl.BlockSpec((None, tm, H * 256), lambda b, t: (b, t, 0)),
                   pl.BlockSpec((None, tm, H * 256), lambda b, t: (b, t, 0)),
                   pl.BlockSpec((None, H, None, MLA_V, tm), lambda b, t: (b, 0, t, 0, 0))],
        out_shape=[jax.ShapeDtypeStruct((B, T, H * 256), BF16),
                   jax.ShapeDtypeStruct((B, T, H * 256), BF16),
                   jax.ShapeDtypeStruct((B, H, nC, MLA_V, tm), BF16)],
        compiler_params=_params(2), name="mla_prep")(
            u, u, rc, rs, q_norm.reshape(1, -1), kv_norm.reshape(1, -1), wq, wk, wvt)


ATTN_SUB = 512
MLA_TQ = 1024
DIFF_TQ = 512
MLA_TKC = 1024
DIFF_TKC = 1024
LOG2E = float(np.log2(np.e))


class _Stream(NamedTuple):
    m: Any
    l: Any
    acc: Any
    s: Any = None
    cm: Any = None
    p: Any = None
    al: Any = None


def _stream_scratch(tq, tkc, dv):
    shapes = [pltpu.VMEM((1, tq), F32), pltpu.VMEM((1, tq), F32), pltpu.VMEM((dv, tq), F32)]
    if tkc:
        shapes += [pltpu.VMEM((2, tkc, tq), F32), pltpu.VMEM((2, 1, tq), F32),
                   pltpu.VMEM((2, tkc, tq), BF16), pltpu.VMEM((2, 1, tq), F32)]
    return shapes


def _load_block(ref, r0, nr, c0, nc):
    return ref[r0:r0 + nr, c0:c0 + nc]


def _softmax_step(k, vt, q, st):
    s = _dot_nt(k(), q())
    m_prev = st.m[...]
    m_new = jnp.maximum(m_prev, jnp.max(s, axis=0, keepdims=True))
    alpha = jnp.exp2(m_prev - m_new)
    p = jnp.exp2(s - m_new)
    st.l[...] = alpha * st.l[...] + jnp.sum(p, axis=0, keepdims=True)
    st.acc[...] = alpha * st.acc[...] + _dot(vt(), p.astype(BF16))
    st.m[...] = m_new


def _stage_scores(k, qs, streams, slot):
    for q, st in zip(qs, streams):
        s = _dot_nt(k(), q())
        st.s[slot] = s
        st.cm[slot] = jnp.max(s, axis=0, keepdims=True)


def _stage_softmax(streams, slot):
    for st in streams:
        m_prev = st.m[...]
        m_new = jnp.maximum(m_prev, st.cm[slot])
        alpha = jnp.exp2(m_prev - m_new)
        p = jnp.exp2(st.s[slot] - m_new)
        st.l[...] = alpha * st.l[...] + jnp.sum(p, axis=0, keepdims=True)
        st.p[slot] = p.astype(BF16)
        st.al[slot] = alpha
        st.m[...] = m_new


def _stage_values(vt, streams, slot):
    for st in streams:
        st.acc[...] = st.al[slot] * st.acc[...] + _dot(vt(), st.p[slot])


def _attend(qs, kc_ref, vtc_ref, kl_ref, vtl_ref, n_lat, tkc, streams):
    for st in streams:
        st.m[...] = jnp.full(st.m.shape, -jnp.inf, F32)
        st.l[...] = jnp.zeros(st.l.shape, F32)
        st.acc[...] = jnp.zeros(st.acc.shape, F32)
    for q, st in zip(qs, streams):
        _softmax_step(lambda: kc_ref[...], lambda: vtc_ref[...], q, st)
    if not n_lat:
        return

    def keys(j):
        return lambda: kl_ref[pl.ds(pl.multiple_of(j * tkc, tkc), tkc), :]

    def vals(j):
        return lambda: vtl_ref[j]

    if n_lat < 4 or n_lat % 2:
        def plain(j, carry):
            for q, st in zip(qs, streams):
                _softmax_step(keys(j), vals(j), q, st)
            return carry
        lax.fori_loop(0, n_lat, plain, 0)
        return

    _stage_scores(keys(0), qs, streams, 0)
    _stage_scores(keys(1), qs, streams, 1)
    _stage_softmax(streams, 0)

    def body(i, carry):
        j = 2 * i
        _stage_scores(keys(j), qs, streams, 0)
        _stage_softmax(streams, 1)
        _stage_values(vals(j - 2), streams, 0)
        _stage_scores(keys(j + 1), qs, streams, 1)
        _stage_softmax(streams, 0)
        _stage_values(vals(j - 1), streams, 1)
        return carry

    lax.fori_loop(1, n_lat // 2, body, 0)
    _stage_softmax(streams, 1)
    _stage_values(vals(n_lat - 2), streams, 0)
    _stage_values(vals(n_lat - 1), streams, 1)


def _split_streams(scratch, n):
    per = len(scratch) // n
    return [_Stream(*scratch[i * per:(i + 1) * per]) for i in range(n)]


def _mla_attn_kernel(*refs, n_lat, tkc, n_sub, ts):
    if n_lat:
        q_ref, kc_ref, vtc_ref, kl_ref, vtl_ref, o_ref = refs[:6]
        scratch = refs[6:]
    else:
        q_ref, kc_ref, vtc_ref, o_ref = refs[:4]
        kl_ref = vtl_ref = None
        scratch = refs[4:]
    streams = _split_streams(scratch, n_sub)
    qs = [functools.partial(_load_block, q_ref, sub * ts, ts, 0, q_ref.shape[1]) for sub in range(n_sub)]
    _attend(qs, kc_ref, vtc_ref, kl_ref, vtl_ref, n_lat, tkc, streams)
    for sub, st in enumerate(streams):
        o = st.acc[...] / st.l[...]
        o_ref[sub * ts:(sub + 1) * ts, :] = o.T.astype(o_ref.dtype)


def _mla_attn(q, k_ctx, vt_ctx, k_lat, vt_lat):
    B, Tq, _ = q.shape
    H = MLA_HEADS
    Tc = k_ctx.shape[1]
    tq = _pick(Tq, MLA_TQ)
    n_lat = 0 if k_lat is None else vt_lat.shape[2]
    tkc = 0 if k_lat is None else vt_lat.shape[4]
    in_specs = [pl.BlockSpec((None, tq, 256), lambda b, h, i: (b, i, h)),
                pl.BlockSpec((None, Tc, 256), lambda b, h, i: (b, 0, h)),
                pl.BlockSpec((None, None, None, MLA_V, Tc), lambda b, h, i: (b, h, 0, 0, 0))]
    args = [q, k_ctx, vt_ctx]
    if n_lat:
        T = k_lat.shape[1]
        in_specs += [pl.BlockSpec((None, T, 256), lambda b, h, i: (b, 0, h)),
                     pl.BlockSpec((None, None, n_lat, MLA_V, tkc), lambda b, h, i: (b, h, 0, 0, 0))]
        args += [k_lat, vt_lat]
    ts = _pick(tq, ATTN_SUB)
    n_sub = tq // ts
    return pl.pallas_call(
        functools.partial(_mla_attn_kernel, n_lat=n_lat, tkc=tkc, n_sub=n_sub, ts=ts),
        grid=(B, H, Tq // tq), in_specs=in_specs,
        out_specs=pl.BlockSpec((None, tq, MLA_V), lambda b, h, i: (b, i, h)),
        out_shape=jax.ShapeDtypeStruct((B, Tq, H * MLA_V), BF16),
        scratch_shapes=_stream_scratch(ts, tkc, MLA_V) * n_sub,
        compiler_params=_params(3), name="mla_attn")(*args)


def _diff_prep_kernel(uq_ref, uk_ref, uv_ref, c_ref, s_ref, eye_ref, q_ref, k_ref, vt_ref, *, rope):
    uq = uq_ref[...] * (DIFF_SCALE * LOG2E)
    uk = uk_ref[...]
    lane = lax.broadcasted_iota(jnp.int32, (uq.shape[0], LANES), 1)
    if rope:
        c = c_ref[...]
        s = s_ref[...]
    for h in range(DIFF_HEADS):
        qh = uq[:, LANES * h:LANES * (h + 1)]
        kh = uk[:, LANES * h:LANES * (h + 1)]
        if rope:
            qh = _rope128(qh, c, s)
            kh = _rope128(kh, c, s)
        q_ref[:, 256 * h:256 * h + 128] = jnp.where(lane < DIFF_QK, qh, 0.0).astype(BF16)
        q_ref[:, 256 * h + 128:256 * h + 256] = jnp.where(lane < DIFF_QK, 0.0, qh).astype(BF16)
        k_ref[:, LANES * h:LANES * (h + 1)] = kh.astype(BF16)
    vt = _dot_nt(eye_ref[...], uv_ref[...].astype(BF16))
    tm = vt.shape[1]
    vt_ref[...] = vt.reshape(DIFF_HEADS, DIFF_V, tm).astype(BF16)


def _diff_prep(u, rc, rs, eye, tkc, rope):
    B, T, _ = u.shape
    tm = tkc
    nC = T // tm
    H = DIFF_HEADS
    return pl.pallas_call(
        functools.partial(_diff_prep_kernel, rope=rope), grid=(B, nC),
        in_specs=[pl.BlockSpec((None, tm, U_BLK), lambda b, t: (b, t, U_CQ)),
                  pl.BlockSpec((None, tm, U_BLK), lambda b, t: (b, t, U_CK)),
                  pl.BlockSpec((None, tm, U_BLK), lambda b, t: (b, t, U_CV)),
                  pl.BlockSpec((tm, LANES), lambda b, t: (t, 0)),
                  pl.BlockSpec((tm, LANES), lambda b, t: (t, 0)),
                  pl.BlockSpec(eye.shape, lambda b, t: (0, 0))],
        out_specs=[pl.BlockSpec((None, tm, H * 256), lambda b, t: (b, t, 0)),
                   pl.BlockSpec((None, tm, H * LANES), lambda b, t: (b, t, 0)),
                   pl.BlockSpec((None, H, None, DIFF_V, tm), lambda b, t: (b, 0, t, 0, 0))],
        out_shape=[jax.ShapeDtypeStruct((B, T, H * 256), BF16),
                   jax.ShapeDtypeStruct((B, T, H * LANES), BF16),
                   jax.ShapeDtypeStruct((B, H, nC, DIFF_V, tm), BF16)],
        compiler_params=_params(2), name="diff_prep")(u, u, u, rc, rs, eye)


def _diff_attn_kernel(*refs, n_lat, tkc, lam_init, n_sub, ts):
    if n_lat:
        q_ref, kc_ref, vtc_ref, kl_ref, vtl_ref, lv_ref, sub_ref, o_ref = refs[:8]
        scratch = refs[8:]
    else:
        q_ref, kc_ref, vtc_ref, lv_ref, sub_ref, o_ref = refs[:6]
        kl_ref = vtl_ref = None
        scratch = refs[6:]
    streams = _split_streams(scratch, 2 * n_sub)
    qs = []
    for sub in range(n_sub):
        qs += [functools.partial(_load_block, q_ref, sub * ts, ts, 0, LANES),
               functools.partial(_load_block, q_ref, sub * ts, ts, LANES, LANES)]
    _attend(qs, kc_ref, vtc_ref, kl_ref, vtl_ref, n_lat, tkc, streams)
    lv = lv_ref[...]
    lam = (jnp.exp(jnp.sum(lv[0:1] * lv[1:2], axis=-1, keepdims=True))
           - jnp.exp(jnp.sum(lv[2:3] * lv[3:4], axis=-1, keepdims=True)) + lam_init)
    for sub in range(n_sub):
        st0, st1 = streams[2 * sub], streams[2 * sub + 1]
        o = st0.acc[...] / st0.l[...] - lam * (st1.acc[...] / st1.l[...])
        ot = o.T
        o_ref[sub * ts:(sub + 1) * ts, :] = (_rms(ot, sub_ref[...]) * (1.0 - lam_init)).astype(o_ref.dtype)


def _diff_attn(q, k_ctx, vt_ctx, k_lat, vt_lat, lam_vecs, subln, lam_init):
    B, Tq, _ = q.shape
    H = DIFF_HEADS
    Tc = k_ctx.shape[1]
    tq = _pick(Tq, DIFF_TQ)
    n_lat = 0 if k_lat is None else vt_lat.shape[2]
    tkc = 0 if k_lat is None else vt_lat.shape[4]
    in_specs = [pl.BlockSpec((None, tq, 256), lambda b, h, i: (b, i, h)),
                pl.BlockSpec((None, Tc, LANES), lambda b, h, i: (b, 0, h)),
                pl.BlockSpec((None, None, None, DIFF_V, Tc), lambda b, h, i: (b, h, 0, 0, 0))]
    args = [q, k_ctx, vt_ctx]
    if n_lat:
        T = k_lat.shape[1]
        in_specs += [pl.BlockSpec((None, T, LANES), lambda b, h, i: (b, 0, h)),
                     pl.BlockSpec((None, None, n_lat, DIFF_V, tkc), lambda b, h, i: (b, h, 0, 0, 0))]
        args += [k_lat, vt_lat]
    in_specs += [pl.BlockSpec((4, DIFF_QK), lambda b, h, i: (0, 0)),
                 pl.BlockSpec((1, DIFF_V), lambda b, h, i: (0, 0))]
    args += [lam_vecs, subln.reshape(1, DIFF_V)]
    ts = _pick(tq, ATTN_SUB)
    n_sub = tq // ts
    scratch = _stream_scratch(ts, tkc, DIFF_V) * (2 * n_sub)
    return pl.pallas_call(
        functools.partial(_diff_attn_kernel, n_lat=n_lat, tkc=tkc, lam_init=lam_init, n_sub=n_sub, ts=ts),
        grid=(B, H, Tq // tq), in_specs=in_specs,
        out_specs=pl.BlockSpec((None, tq, DIFF_V), lambda b, h, i: (b, i, h)),
        out_shape=jax.ShapeDtypeStruct((B, Tq, H * DIFF_V), BF16),
        scratch_shapes=scratch, compiler_params=_params(3), name="diff_attn")(*args)


def _gelu_tanh(x):
    return 0.5 * x * (1.0 + jnp.tanh(np.sqrt(2.0 / np.pi) * (x + 0.044715 * (x * x * x))))


def _rglru_kernel(*refs, rev, tb, nt, final):
    if final:
        (x_ref, xp_ref, xn_ref, cw_ref, cb_ref, wri_ref, bri_ref, lam_ref, h0_ref, gate_ref, hf_ref,
         o_ref, ext_ref, carry_ref) = refs
    else:
        (x_ref, xp_ref, xn_ref, cw_ref, cb_ref, wri_ref, bri_ref, lam_ref, h0_ref,
         o_ref, ext_ref, carry_ref) = refs
    i = pl.program_id(1)
    blk = (nt - 1 - i) if rev else i

    @pl.when(i == 0)
    def _():
        carry_ref[...] = h0_ref[...]

    ext_ref[0:SUBLANES, :] = jnp.where(blk > 0, xp_ref[...], 0.0)
    ext_ref[SUBLANES:SUBLANES + tb, :] = x_ref[...]
    ext_ref[SUBLANES + tb:2 * SUBLANES + tb, :] = jnp.where(blk < nt - 1, xn_ref[...], 0.0)
    left = RG_CONV // 2
    xc = cb_ref[...]
    for tap in range(RG_CONV):
        off = SUBLANES - left + tap
        xc = xc + cw_ref[tap:tap + 1, :] * ext_ref[off:off + tb, :]
    xcb = xc.astype(BF16)

    rs, gs = [], []
    for kb in range(RG_BLOCKS):
        ri = _dot(xcb[:, RG_BLOCK * kb:RG_BLOCK * (kb + 1)], wri_ref[kb])
        rs.append(ri[:, :RG_BLOCK])
        gs.append(ri[:, RG_BLOCK:])
    bri = bri_ref[...]
    r = _sigmoid(jnp.concatenate(rs, axis=1) + bri[0:1])
    ig = _sigmoid(jnp.concatenate(gs, axis=1) + bri[1:2])
    lam = lam_ref[...]
    softplus_neg = jnp.maximum(-lam, 0.0) + jnp.log(1.0 + jnp.exp(-jnp.abs(lam)))
    log_a = -RG_C * r * softplus_neg
    a = jnp.exp(log_a)
    bv = jnp.sqrt(-jnp.tanh(log_a) * (a * a + 1.0)) * (ig * xc)

    row = lax.broadcasted_iota(jnp.int32, a.shape, 0)
    d = 1
    while d < tb:
        if rev:
            keep = row < tb - d
            a_sh = pltpu.roll(a, tb - d, 0)
            b_sh = pltpu.roll(bv, tb - d, 0)
        else:
            keep = row >= d
            a_sh = pltpu.roll(a, d, 0)
            b_sh = pltpu.roll(bv, d, 0)
        bv = a * jnp.where(keep, b_sh, 0.0) + bv
        a = a * jnp.where(keep, a_sh, 1.0)
        d *= 2
    h = a * carry_ref[...] + bv
    last = 0 if rev else tb - 1
    carry_ref[...] = h[last:last + 1, :]
    if final:
        o_ref[...] = ((hf_ref[...] + h) * _gelu_tanh(gate_ref[...])).astype(o_ref.dtype)
    else:
        o_ref[...] = h


def _rglru_pass(u, conv_w, conv_b, wri, bri, lam, h0, hf, rev):
    B, T, _ = u.shape
    W = RG_WIDTH
    tb = _pick(T, 256)
    nt = T // tb
    r8 = tb // SUBLANES
    final = hf is not None

    def tix(i):
        return (nt - 1 - i) if rev else i

    in_specs = [pl.BlockSpec((None, tb, U_BLK), lambda b, i: (b, tix(i), U_BX)),
                pl.BlockSpec((None, SUBLANES, U_BLK),
                             lambda b, i: (b, jnp.maximum(tix(i) * r8 - 1, 0), U_BX)),
                pl.BlockSpec((None, SUBLANES, U_BLK),
                             lambda b, i: (b, jnp.minimum((tix(i) + 1) * r8, T // SUBLANES - 1), U_BX)),
                pl.BlockSpec((RG_CONV, W), lambda b, i: (0, 0)),
                pl.BlockSpec((1, W), lambda b, i: (0, 0)),
                pl.BlockSpec((RG_BLOCKS, RG_BLOCK, 2 * RG_BLOCK), lambda b, i: (0, 0, 0)),
                pl.BlockSpec((2, W), lambda b, i: (0, 0)),
                pl.BlockSpec((1, W), lambda b, i: (0, 0)),
                pl.BlockSpec((None, 1, W), lambda b, i: (b, 0, 0))]
    args = [u, u, u, conv_w, conv_b.reshape(1, W), wri, bri, lam.reshape(1, W), h0]
    if final:
        in_specs += [pl.BlockSpec((None, tb, U_BLK), lambda b, i: (b, tix(i), U_BG)),
                     pl.BlockSpec((None, tb, W), lambda b, i: (b, tix(i), 0))]
        args += [u, hf]
    return pl.pallas_call(
        functools.partial(_rglru_kernel, rev=rev, tb=tb, nt=nt, final=final), grid=(B, nt),
        in_specs=in_specs,
        out_specs=pl.BlockSpec((None, tb, W), lambda b, i: (b, tix(i), 0)),
        out_shape=jax.ShapeDtypeStruct((B, T, W), BF16 if final else F32),
        scratch_shapes=[pltpu.VMEM((tb + 2 * SUBLANES, W), F32), pltpu.VMEM((1, W), F32)],
        compiler_params=_params(2), name="rglru_bwd" if rev else "rglru_fwd")(*args)


HGRN_CHUNK = 128
HGRN_SUB = 8
HGRN_PAIR = 2


def _hgrn_chunk(qv, z, v, lb, st, ones, rev):
    C = HGRN_CHUNK
    sg = _sigmoid(z)
    k = (1.0 - lb) * _sigmoid(-z)
    g = jnp.log(lb + (1.0 - lb) * sg)
    row = lax.broadcasted_iota(jnp.int32, (C, LANES), 0)
    col = lax.broadcasted_iota(jnp.int32, (C, C), 1)
    rowc = lax.broadcasted_iota(jnp.int32, (C, C), 0)

    G = g
    d = 1
    while d < C:
        if rev:
            G = G + jnp.where(row < C - d, pltpu.roll(G, C - d, 0), 0.0)
        else:
            G = G + jnp.where(row >= d, pltpu.roll(G, d, 0), 0.0)
        d *= 2
    G = G * LOG2E
    last = 0 if rev else C - 1
    g_last = G[last:last + 1, :]

    qe = (qv * jnp.exp2(G)).astype(BF16)
    o = _dot_nt(qe, st.astype(BF16))
    kdec = (k * jnp.exp2(g_last - G)).astype(BF16)
    vb = v.astype(BF16)
    st_new = st * jnp.exp2(g_last) + _dot(v.T.astype(BF16), kdec)

    pos = (C - 1 - row) if rev else row
    a_sum = jnp.zeros((C, C), F32)
    hsz = HGRN_SUB
    while hsz < C:
        grp = 2 * hsz
        ng = C // grp
        G3 = G.reshape(ng, grp, LANES)
        ref = G3[:, hsz:hsz + 1, :] if rev else G3[:, hsz - 1:hsz, :]
        ref = jnp.broadcast_to(ref, (ng, grp, LANES)).reshape(C, LANES)
        e = jnp.exp2(-jnp.abs(G - ref))
        later = ((pos >> int(np.log2(hsz))) & 1) == 1
        ql = jnp.where(later, qv * e, 0.0).astype(BF16)
        kl = jnp.where(later, 0.0, k * e).astype(BF16)
        al = _dot_nt(ql, kl)
        shift = int(np.log2(grp))
        a_sum = a_sum + jnp.where((rowc >> shift) == (col >> shift), al, 0.0)
        hsz = grp
    o = o + _dot(a_sum.astype(BF16), vb)

    nb = C // HGRN_SUB
    q3 = qv.reshape(nb, HGRN_SUB, LANES)
    k3 = k.reshape(nb, HGRN_SUB, LANES)
    v3 = v.reshape(nb, HGRN_SUB, LANES)
    G3 = G.reshape(nb, HGRN_SUB, LANES)
    rb = lax.broadcasted_iota(jnp.int32, (nb, HGRN_SUB, LANES), 1)
    od = jnp.zeros((nb, HGRN_SUB, LANES), F32)
    for s in range(HGRN_SUB):
        valid = (rb <= s) if rev else (rb >= s)
        e = jnp.exp2(G3 - G3[:, s:s + 1, :])
        p = jnp.where(valid, q3 * k3[:, s:s + 1, :] * e, 0.0)
        rs = _dot(p.reshape(C, LANES).astype(BF16), ones)
        od = od + rs.reshape(nb, HGRN_SUB, LANES) * v3[:, s:s + 1, :]
    return o + od.reshape(C, LANES), st_new


def _hgrn_kernel(*refs, rev, tb, nt, final):
    if final:
        (q_ref, z_ref, v_ref, lb_ref, s0_ref, ones_ref, of_ref, og_ref, gn_ref,
         o_ref, sfin_ref, st_ref) = refs
    else:
        q_ref, z_ref, v_ref, lb_ref, s0_ref, ones_ref, o_ref, sfin_ref, st_ref = refs
    i = pl.program_id(2)

    @pl.when(i == 0)
    def _():
        st_ref[...] = s0_ref[...]

    ones = ones_ref[...]
    C = HGRN_CHUNK
    nch = tb // C

    def body(c, carry):
        cc = (nch - 1 - c) if rev else c
        r0 = pl.multiple_of(cc * C, C)
        for hh in range(HGRN_PAIR):
            lanes = slice(hh * LANES, (hh + 1) * LANES)
            qin = q_ref[pl.ds(r0, C), lanes]
            qv = qin * _sigmoid(qin)
            o, st_new = _hgrn_chunk(qv, z_ref[pl.ds(r0, C), lanes], v_ref[pl.ds(r0, C), lanes],
                                    lb_ref[:, lanes], st_ref[hh], ones, rev)
            st_ref[hh] = st_new
            if final:
                o = o + of_ref[pl.ds(r0, C), lanes]
                og = og_ref[pl.ds(r0, C), lanes]
                o = _rms(o, gn_ref[...]) * (og * _sigmoid(og))
            o_ref[pl.ds(r0, C), lanes] = o.astype(o_ref.dtype)
        return carry

    lax.fori_loop(0, nch, body, 0, unroll=True)

    @pl.when(i == nt - 1)
    def _():
        sfin_ref[...] = st_ref[...]


def _hgrn_pass(u, z_blk, lb, s0, ones, of, g_norm, rev):
    B, T, _ = u.shape
    H = HGRN_HEADS
    tb = _pick(T, 512)
    nt = T // tb
    final = of is not None

    def tix(i):
        return (nt - 1 - i) if rev else i

    P = HGRN_PAIR
    W = P * LANES

    def ublk(blk):
        return pl.BlockSpec((None, tb, W), lambda b, h, i: (b, tix(i), blk * (U_BLK // W) + h))

    in_specs = [ublk(U_DQ), ublk(z_blk), ublk(U_DI),
                pl.BlockSpec((1, W), lambda b, h, i: (0, h)),
                pl.BlockSpec((None, P, HGRN_V, HGRN_K), lambda b, h, i: (b, h, 0, 0)),
                pl.BlockSpec((LANES, LANES), lambda b, h, i: (0, 0))]
    args = [u, u, u, lb.reshape(1, -1), s0, ones]
    if final:
        in_specs += [pl.BlockSpec((None, tb, W), lambda b, h, i: (b, tix(i), h)),
                     ublk(U_DG),
                     pl.BlockSpec((1, HGRN_V), lambda b, h, i: (0, 0))]
        args += [of, u, g_norm.reshape(1, HGRN_V)]
    return pl.pallas_call(
        functools.partial(_hgrn_kernel, rev=rev, tb=tb, nt=nt, final=final), grid=(B, H // P, nt),
        in_specs=in_specs,
        out_specs=[pl.BlockSpec((None, tb, W), lambda b, h, i: (b, tix(i), h)),
                   pl.BlockSpec((None, P, HGRN_V, HGRN_K), lambda b, h, i: (b, h, 0, 0))],
        out_shape=[jax.ShapeDtypeStruct((B, T, H * HGRN_V), BF16 if final else F32),
                   jax.ShapeDtypeStruct((B, H, HGRN_V, HGRN_K), F32)],
        scratch_shapes=[pltpu.VMEM((P, HGRN_V, HGRN_K), F32)],
        compiler_params=_params(3), name="hgrn_bwd" if rev else "hgrn_fwd")(*args)


def _merge_kernel(x_ref, mod_ref, pre_ref, post_ref, ya_ref, yb_ref, yc_ref, yd_ref, w_ref, wb_ref, mb_ref,
                  o_ref, h_ref, acc_ref):
    n = pl.program_id(2)

    @pl.when(n == 0)
    def _():
        h_ref[...] = _prenorm(x_ref[...], mod_ref, 1, pre_ref[...]).astype(BF16)
        acc_ref[...] = jnp.zeros(acc_ref.shape, F32)

    for nb, y_ref in enumerate((ya_ref, yb_ref, yc_ref, yd_ref)):
        @pl.when(n == nb)
        def _(y_ref=y_ref):
            gate = _sigmoid(_dot(h_ref[...], w_ref[...]) + mb_ref[...])
            acc_ref[...] += gate * _dot(y_ref[...], wb_ref[...])

    @pl.when(n == N_BRANCH)
    def _():
        mix = _dot(acc_ref[...].astype(BF16), w_ref[...])
        o_ref[...] = x_ref[...] + mod_ref[5:6, :] * _rms(mix, post_ref[...])


def _merge(x, mod, pre_g, post_g, ys, w5, wb, mb, l):
    B, T, D = x.shape
    tm = _pick(T, 512)
    last = N_BRANCH - 1
    yspec = pl.BlockSpec((None, tm, BRANCH_W), lambda b, t, n: (b, t, 0))
    return pl.pallas_call(
        _merge_kernel, grid=(B, T // tm, N_BRANCH + 1),
        in_specs=[pl.BlockSpec((None, tm, D), lambda b, t, n: (b, t, 0)),
                  pl.BlockSpec((None, N_MOD, D), lambda b, t, n: (b, 0, 0)),
                  pl.BlockSpec((1, D), lambda b, t, n: (0, 0)),
                  pl.BlockSpec((1, D), lambda b, t, n: (0, 0)),
                  yspec, yspec, yspec, yspec,
                  pl.BlockSpec((D, D), lambda b, t, n: (0, n)),
                  pl.BlockSpec((None, None, BRANCH_W, D), lambda b, t, n: (l, jnp.minimum(n, last), 0, 0)),
                  pl.BlockSpec((None, None, 1, D), lambda b, t, n: (l, jnp.minimum(n, last), 0, 0))],
        out_specs=pl.BlockSpec((None, tm, D), lambda b, t, n: (b, t, 0)),
        out_shape=jax.ShapeDtypeStruct(x.shape, F32),
        scratch_shapes=[pltpu.VMEM((tm, D), BF16), pltpu.VMEM((tm, D), F32)],
        compiler_params=_params(3), name="merge")(
            x, mod, pre_g.reshape(1, D), post_g.reshape(1, D), *ys, w5, wb, mb)


def _rope_tables(n_tok):
    axis_dim = MLA_ROPE // 2
    inv_freq = ROPE_BASE ** (-jnp.arange(0, axis_dim, 2, dtype=F32) / axis_dim)
    tok = lax.broadcasted_iota(jnp.int32, (n_tok, LANES), 0)
    lane = lax.broadcasted_iota(jnp.int32, (n_tok, LANES), 1)
    pos = jnp.where((lane & 32) == 0, tok // GRID_W, tok % GRID_W).astype(F32)
    ang = pos * jnp.tile(inv_freq, LANES // inv_freq.shape[0])[None, :]
    cos = jnp.cos(ang)
    sin = jnp.where((lane & 16) == 0, -jnp.sin(ang), jnp.sin(ang))
    first = lane < MLA_ROPE
    return cos, sin, jnp.where(first, cos, 1.0), jnp.where(first, sin, 0.0)


def _layer_weights(l, w_in, mla_w_uq, mla_w_ukv, rg_w_r, rg_w_i, rg_b_r, rg_b_i, w_out):
    D = D_MODEL
    cuts = np.cumsum([MLA_Q_LORA, MLA_KV_LORA, MLA_ROPE, RG_WIDTH, RG_WIDTH, 512, 512, 512,
                      512, 512, 512, 512, 512])
    wi = w_in[l]
    parts = jnp.split(wi[:, :cuts[-1]], cuts[:-1].tolist(), axis=1)
    (a_q, a_kv, a_kr, b_x, b_g, c_q, c_k, c_v, d_q, d_ff, d_fb, d_i, d_g) = parts
    pad = jnp.zeros((D, U_BLK - MLA_KV_LORA - MLA_ROPE), wi.dtype)
    w_u = jnp.concatenate([a_q, b_x, b_g, c_q, c_k, c_v, d_q, d_ff, d_fb, d_i, d_g, a_kv, a_kr, pad],
                          axis=1).astype(BF16)
    w5 = jnp.concatenate([wi[:, cuts[-1]:], w_out[l]], axis=1).astype(BF16)

    wq = mla_w_uq[l].reshape(MLA_Q_LORA, MLA_HEADS, MLA_NOPE + MLA_ROPE)
    wq = jnp.pad(wq, ((0, 0), (0, 0), (0, 256 - MLA_NOPE - MLA_ROPE))).reshape(MLA_Q_LORA, -1).astype(BF16)
    wkv = mla_w_ukv[l].reshape(MLA_KV_LORA, MLA_HEADS, MLA_NOPE + MLA_V)
    wk = wkv[:, :, :MLA_NOPE].reshape(MLA_KV_LORA, -1).astype(BF16)
    wvt = wkv[:, :, MLA_NOPE:].reshape(MLA_KV_LORA, -1).T.astype(BF16)

    wri = [jnp.concatenate([rg_w_r[l, d], rg_w_i[l, d]], axis=-1).astype(BF16) for d in range(2)]
    bri = [jnp.stack([rg_b_r[l, d], rg_b_i[l, d]], axis=0) for d in range(2)]
    return w_u, w5, wq, wk, wvt, wri, bri


def kernel(x, c, ctx, c_ctx, ada_w, ada_b, pre_norm, post_norm, ffn_w_gate, ffn_w_up, ffn_w_down, w_in, mla_q_norm, mla_w_uq, mla_kv_norm, mla_w_ukv, rg_conv_w, rg_conv_b, rg_w_r, rg_b_r, rg_w_i, rg_b_i, rg_lambda, diff_lambda, diff_subln, hgrn_lb_logits, hgrn_norm, merge_b, w_branch, w_out):
    B, S, D = x.shape
    Tc = ctx.shape[1]
    depth = ada_w.shape[0]
    tkc_mla = _pick(S, MLA_TKC)
    tkc_diff = _pick(S, DIFF_TKC)

    rc_diff, rs_diff, rc_mla, rs_mla = _rope_tables(S)
    rc_ctx = jnp.ones((Tc, LANES), F32)
    rs_ctx = jnp.zeros((Tc, LANES), F32)

    lb_soft = jax.nn.softmax(hgrn_lb_logits.astype(F32), axis=0)
    hgrn_lb = jnp.cumsum(lb_soft, axis=0) - lb_soft[0]

    c8 = jnp.zeros((SUBLANES, D), F32).at[:B].set(c).at[B].set(c_ctx)
    eye = jnp.eye(DIFF_HEADS * DIFF_V, dtype=BF16)
    ones = jnp.ones((LANES, LANES), BF16)
    zero_state = jnp.zeros((B, HGRN_HEADS, HGRN_V, HGRN_K), F32)
    zero_h = jnp.zeros((B, 1, RG_WIDTH), F32)

    wg = ffn_w_gate.astype(BF16)
    wu = ffn_w_up.astype(BF16)
    wd = ffn_w_down.astype(BF16)
    wb = w_branch.astype(BF16)
    mb = merge_b.reshape(depth, N_BRANCH, 1, D)

    xc = ctx.reshape(1, B * Tc, D)
    for l in range(depth):
        need_ctx = l < depth - 1
        lam_init = 0.8 - 0.6 * float(np.exp(-0.3 * l))
        mod8 = _ada_mod(c8, ada_w, ada_b, l).reshape(SUBLANES, N_MOD, D)
        mod = mod8[:B]
        mod_c = mod8[B:B + 1]
        w_u, w5, wq, wk, wvt, wri, bri = _layer_weights(
            l, w_in, mla_w_uq, mla_w_ukv, rg_w_r, rg_w_i, rg_b_r, rg_b_i, w_out)

        x = _ffn(x, mod, 0, pre_norm[l, 0], post_norm[l, 0], wg, wu, wd, l, 0)
        xc = _ffn(xc, mod_c, 0, pre_norm[l, 0], post_norm[l, 0], wg, wu, wd, l, 0)

        u = _inproj(x, mod, pre_norm[l, 1], w_u)
        uc = _inproj(xc, mod_c, pre_norm[l, 1], w_u).reshape(B, Tc, -1)

        qa_c, ka_c, vta_c = _mla_prep(uc, rc_ctx, rs_ctx, mla_q_norm[l], mla_kv_norm[l], wq, wk, wvt, Tc, False)
        qa, ka, vta = _mla_prep(u, rc_mla, rs_mla, mla_q_norm[l], mla_kv_norm[l], wq, wk, wvt, tkc_mla, True)
        y_a = _mla_attn(qa, ka_c, vta_c, ka, vta)

        qd_c, kd_c, vtd_c = _diff_prep(uc, rc_ctx, rs_ctx, eye, Tc, False)
        qd, kd, vtd = _diff_prep(u, rc_diff, rs_diff, eye, tkc_diff, True)
        y_c = _diff_attn(qd, kd_c, vtd_c, kd, vtd, diff_lambda[l], diff_subln[l], lam_init)

        hc_f = _rglru_pass(uc, rg_conv_w[l], rg_conv_b[l], wri[0], bri[0], rg_lambda[l, 0], zero_h, None, False)
        hl_f = _rglru_pass(u, rg_conv_w[l], rg_conv_b[l], wri[0], bri[0], rg_lambda[l, 0],
                           hc_f[:, Tc - 1:Tc, :], None, False)
        if need_ctx:
            y_b_c = _rglru_pass(uc, rg_conv_w[l], rg_conv_b[l], wri[1], bri[1], rg_lambda[l, 1], zero_h, hc_f, True)
        hc_b = _rglru_pass(uc, rg_conv_w[l], rg_conv_b[l], wri[1], bri[1], rg_lambda[l, 1], zero_h, None, True)
        y_b = _rglru_pass(u, rg_conv_w[l], rg_conv_b[l], wri[1], bri[1], rg_lambda[l, 1],
                          hc_b[:, 0:1, :], hl_f, True)

        oc_f, s_f = _hgrn_pass(uc, U_DFF, hgrn_lb[l], zero_state, ones, None, None, False)
        if need_ctx:
            y_d_c, s_b = _hgrn_pass(uc, U_DFB, hgrn_lb[l], zero_state, ones, oc_f, hgrn_norm[l], True)
        else:
            _, s_b = _hgrn_pass(uc, U_DFB, hgrn_lb[l], zero_state, ones, None, None, True)
        o_f, _ = _hgrn_pass(u, U_DFF, hgrn_lb[l], s_f, ones, None, None, False)
        y_d, _ = _hgrn_pass(u, U_DFB, hgrn_lb[l], s_b, ones, o_f, hgrn_norm[l], True)

        x = _merge(x, mod, pre_norm[l, 1], post_norm[l, 1], (y_a, y_b, y_c, y_d), w5, wb, mb, l)
        if need_ctx:
            y_a_c = _mla_attn(qa_c, ka_c, vta_c, None, None)
            y_c_c = _diff_attn(qd_c, kd_c, vtd_c, None, None, diff_lambda[l], diff_subln[l], lam_init)
            ys_c = tuple(y.reshape(1, B * Tc, BRANCH_W) for y in (y_a_c, y_b_c, y_c_c, y_d_c))
            xc = _merge(xc, mod_c, pre_norm[l, 1], post_norm[l, 1], ys_c, w5, wb, mb, l)
            xc = _ffn(xc, mod_c, 2, pre_norm[l, 2], post_norm[l, 2], wg, wu, wd, l, 1)

        x = _ffn(x, mod, 2, pre_norm[l, 2], post_norm[l, 2], wg, wu, wd, l, 1)
    return x
```

```python
import functools
from typing import Any, NamedTuple

import jax
import jax.numpy as jnp
import numpy as np
from jax import lax
from jax.experimental import pallas as pl
from jax.experimental.pallas import tpu as pltpu

F32 = jnp.float32
BF16 = jnp.bfloat16

D_MODEL = 2048
GRID_W = 64
N_SUB = 3
N_MOD = 3 * N_SUB
FFN_RES = 0.5
NORM_EPS = 1e-6
ROPE_BASE = 10000.0

MLA_HEADS = 4
MLA_Q_LORA = 512
MLA_KV_LORA = 256
MLA_NOPE = 128
MLA_ROPE = 64
MLA_V = 128
MLA_SCALE = (MLA_NOPE + MLA_ROPE) ** -0.5

RG_WIDTH = 512
RG_BLOCKS = 4
RG_BLOCK = RG_WIDTH // RG_BLOCKS
RG_CONV = 4
RG_C = 8.0

DIFF_HEADS = 4
DIFF_QK = 64
DIFF_V = 2 * DIFF_QK
DIFF_SCALE = DIFF_QK ** -0.5

HGRN_HEADS = 4
HGRN_K = 128
HGRN_V = 128

N_BRANCH = 4
BRANCH_W = 512

LANES = 128
SUBLANES = 8
VMEM_LIMIT_BYTES = 60 * 1024 * 1024
FFN_TM = 1024
FFN_TF = 256

NT_DIMS = (((1,), (1,)), ((), ()))


def _params(n_axes):
    return pltpu.CompilerParams(dimension_semantics=("arbitrary",) * n_axes,
                                vmem_limit_bytes=VMEM_LIMIT_BYTES)


def _pick(n, pref):
    t = min(n, pref)
    assert n % t == 0, (n, pref)
    return t


def _rms(x, g):
    return x * lax.rsqrt(jnp.mean(x * x, axis=-1, keepdims=True) + NORM_EPS) * g


def _sigmoid(x):
    return jax.nn.sigmoid(x)


def _dot(a, b):
    return jnp.dot(a, b, preferred_element_type=F32)


def _dot_nt(a, b):
    return lax.dot_general(a, b, NT_DIMS, preferred_element_type=F32)


def _ada_kernel(c_ref, w_ref, b_ref, o_ref):
    c = c_ref[...]
    cond = (c * _sigmoid(c)).astype(BF16)
    o_ref[...] = _dot(cond, w_ref[...].astype(BF16)) + b_ref[...]


def _ada_mod(c8, w, b, l):
    L, D, N = w.shape
    tn = _pick(N, 2048)
    return pl.pallas_call(
        _ada_kernel, grid=(N // tn,),
        in_specs=[pl.BlockSpec((SUBLANES, D), lambda n: (0, 0)),
                  pl.BlockSpec((None, D, tn), lambda n: (l, 0, n)),
                  pl.BlockSpec((None, 1, tn), lambda n: (l, 0, n))],
        out_specs=pl.BlockSpec((SUBLANES, tn), lambda n: (0, n)),
        out_shape=jax.ShapeDtypeStruct((SUBLANES, N), F32),
        compiler_params=_params(1), name="ada_mod")(c8, w, b.reshape(L, 1, N))


def _prenorm(x, mod_ref, j, g):
    return _rms(x, g) * (1.0 + mod_ref[3 * j + 1:3 * j + 2, :]) + mod_ref[3 * j:3 * j + 1, :]


def _ffn_kernel(x_ref, mod_ref, pre_ref, post_ref, wg_ref, wu_ref, wd_ref, o_ref, h_ref, *, j, nf):
    f = pl.program_id(2)

    @pl.when(f == 0)
    def _():
        h_ref[...] = _prenorm(x_ref[...], mod_ref, j, pre_ref[...]).astype(BF16)
        o_ref[...] = jnp.zeros(o_ref.shape, F32)

    h = h_ref[...]
    g = _dot(h, wg_ref[...])
    u = _dot(h, wu_ref[...])
    a = (g * _sigmoid(g) * u).astype(BF16)
    o_ref[...] += _dot(a, wd_ref[...])

    @pl.when(f == nf - 1)
    def _():
        yn = _rms(o_ref[...], post_ref[...])
        o_ref[...] = x_ref[...] + FFN_RES * mod_ref[3 * j + 2:3 * j + 3, :] * yn


def _ffn(x, mod, j, pre_g, post_g, wg, wu, wd, l, i):
    B, T, D = x.shape
    F = wg.shape[-1]
    tm = _pick(T, FFN_TM)
    tf = _pick(F, FFN_TF)
    nf = F // tf
    return pl.pallas_call(
        functools.partial(_ffn_kernel, j=j, nf=nf), grid=(B, T // tm, nf),
        in_specs=[pl.BlockSpec((None, tm, D), lambda b, t, f: (b, t, 0)),
                  pl.BlockSpec((None, N_MOD, D), lambda b, t, f: (b, 0, 0)),
                  pl.BlockSpec((1, D), lambda b, t, f: (0, 0)),
                  pl.BlockSpec((1, D), lambda b, t, f: (0, 0)),
                  pl.BlockSpec((None, None, D, tf), lambda b, t, f: (l, i, 0, f)),
                  pl.BlockSpec((None, None, D, tf), lambda b, t, f: (l, i, 0, f)),
                  pl.BlockSpec((None, None, tf, D), lambda b, t, f: (l, i, f, 0))],
        out_specs=pl.BlockSpec((None, tm, D), lambda b, t, f: (b, t, 0)),
        out_shape=jax.ShapeDtypeStruct(x.shape, F32),
        scratch_shapes=[pltpu.VMEM((tm, D), BF16)],
        compiler_params=_params(3), name="ffn")(
            x, mod, pre_g.reshape(1, D), post_g.reshape(1, D), wg, wu, wd)


def _inproj_kernel(x_ref, mod_ref, pre_ref, w_ref, o_ref, h_ref):
    @pl.when(pl.program_id(2) == 0)
    def _():
        h_ref[...] = _prenorm(x_ref[...], mod_ref, 1, pre_ref[...]).astype(BF16)

    o_ref[...] = _dot(h_ref[...], w_ref[...])


def _inproj(x, mod, pre_g, w):
    B, T, D = x.shape
    N = w.shape[1]
    tm = _pick(T, 1024)
    tn = _pick(N, 512)
    return pl.pallas_call(
        _inproj_kernel, grid=(B, T // tm, N // tn),
        in_specs=[pl.BlockSpec((None, tm, D), lambda b, t, n: (b, t, 0)),
                  pl.BlockSpec((None, N_MOD, D), lambda b, t, n: (b, 0, 0)),
                  pl.BlockSpec((1, D), lambda b, t, n: (0, 0)),
                  pl.BlockSpec((D, tn), lambda b, t, n: (0, n))],
        out_specs=pl.BlockSpec((None, tm, tn), lambda b, t, n: (b, t, n)),
        out_shape=jax.ShapeDtypeStruct((B, T, N), F32),
        scratch_shapes=[pltpu.VMEM((tm, D), BF16)],
        compiler_params=_params(3), name="inproj")(x, mod, pre_g.reshape(1, D), w)


U_AQ, U_BX, U_BG, U_CQ, U_CK, U_CV, U_DQ, U_DFF, U_DFB, U_DI, U_DG, U_AKV = range(12)
U_BLK = 512


def _rope128(x, c, s):
    lane = lax.broadcasted_iota(jnp.int32, x.shape, 1)
    partner = jnp.where((lane & 31) < 16, pltpu.roll(x, LANES - 16, 1), pltpu.roll(x, 16, 1))
    return x * c + partner * s


def _mla_prep_kernel(uq_ref, ukv_ref, c_ref, s_ref, qn_ref, kvn_ref, wq_ref, wk_ref, wvt_ref,
                     q_ref, k_ref, vt_ref, *, rope):
    uqn = _rms(uq_ref[...], qn_ref[...]).astype(BF16)
    q = _dot(uqn, wq_ref[...]) * (MLA_SCALE * LOG2E)
    ukv = ukv_ref[...]
    ukvn = _rms(ukv[:, :MLA_KV_LORA], kvn_ref[...]).astype(BF16)
    kn = _dot(ukvn, wk_ref[...])
    kr = ukv[:, MLA_KV_LORA:MLA_KV_LORA + LANES]
    if rope:
        c = c_ref[...]
        s = s_ref[...]
        kr = _rope128(kr, c, s)
    krb = kr.astype(BF16)
    for h in range(MLA_HEADS):
        qr = q[:, 256 * h + 128:256 * h + 256]
        if rope:
            qr = _rope128(qr, c, s)
        q_ref[:, 256 * h:256 * h + 128] = q[:, 256 * h:256 * h + 128].astype(BF16)
        q_ref[:, 256 * h + 128:256 * h + 256] = qr.astype(BF16)
        k_ref[:, 256 * h:256 * h + 128] = kn[:, 128 * h:128 * h + 128].astype(BF16)
        k_ref[:, 256 * h + 128:256 * h + 256] = krb
    vt = _dot_nt(wvt_ref[...], ukvn)
    tm = vt.shape[1]
    vt_ref[...] = vt.reshape(MLA_HEADS, MLA_V, tm).astype(BF16)


def _mla_prep(u, rc, rs, q_norm, kv_norm, wq, wk, wvt, tkc, rope):
    B, T, _ = u.shape
    tm = tkc
    nC = T // tm
    H = MLA_HEADS
    return pl.pallas_call(
        functools.partial(_mla_prep_kernel, rope=rope), grid=(B, nC),
        in_specs=[pl.BlockSpec((None, tm, U_BLK), lambda b, t: (b, t, U_AQ)),
                  pl.BlockSpec((None, tm, U_BLK), lambda b, t: (b, t, U_AKV)),
                  pl.BlockSpec((tm, LANES), lambda b, t: (t, 0)),
                  pl.BlockSpec((tm, LANES), lambda b, t: (t, 0)),
                  pl.BlockSpec((1, MLA_Q_LORA), lambda b, t: (0, 0)),
                  pl.BlockSpec((1, MLA_KV_LORA), lambda b, t: (0, 0)),
                  pl.BlockSpec(wq.shape, lambda b, t: (0, 0)),
                  pl.BlockSpec(wk.shape, lambda b, t: (0, 0)),
                  pl.BlockSpec(wvt.shape, lambda b, t: (0, 0))],
        out_specs=[pl.BlockSpec((None, tm, H * 256), lambda b, t: (b, t, 0)),
                   pl.BlockSpec((None, tm, H * 256), lambda b, t: (b, t, 0)),
                   pl.BlockSpec((None, H, None, MLA_V, tm), lambda b, t: (b, 0, t, 0, 0))],
        out_shape=[jax.ShapeDtypeStruct((B, T, H * 256), BF16),
                   jax.ShapeDtypeStruct((B, T, H * 256), BF16),
                   jax.ShapeDtypeStruct((B, H, nC, MLA_V, tm), BF16)],
        compiler_params=_params(2), name="mla_prep")(
            u, u, rc, rs, q_norm.reshape(1, -1), kv_norm.reshape(1, -1), wq, wk, wvt)


ATTN_SUB = 512
MLA_TQ = 1024
DIFF_TQ = 512
MLA_TKC = 1024
DIFF_TKC = 1024
LOG2E = float(np.log2(np.e))


class _Stream(NamedTuple):
    m: Any
    l: Any
    acc: Any
    s: Any = None
    cm: Any = None
    p: Any = None
    al: Any = None


def _stream_scratch(tq, tkc, dv):
    shapes = [pltpu.VMEM((1, tq), F32), pltpu.VMEM((1, tq), F32), pltpu.VMEM((dv, tq), F32)]
    if tkc:
        shapes += [pltpu.VMEM((2, tkc, tq), F32), pltpu.VMEM((2, 1, tq), F32),
                   pltpu.VMEM((2, tkc, tq), BF16), pltpu.VMEM((2, 1, tq), F32)]
    return shapes


def _load_block(ref, r0, nr, c0, nc):
    return ref[r0:r0 + nr, c0:c0 + nc]


def _softmax_step(k, vt, q, st):
    s = _dot_nt(k(), q())
    m_prev = st.m[...]
    m_new = jnp.maximum(m_prev, jnp.max(s, axis=0, keepdims=True))
    alpha = jnp.exp2(m_prev - m_new)
    p = jnp.exp2(s - m_new)
    st.l[...] = alpha * st.l[...] + jnp.sum(p, axis=0, keepdims=True)
    st.acc[...] = alpha * st.acc[...] + _dot(vt(), p.astype(BF16))
    st.m[...] = m_new


def _stage_scores(k, qs, streams, slot):
    for q, st in zip(qs, streams):
        s = _dot_nt(k(), q())
        st.s[slot] = s
        st.cm[slot] = jnp.max(s, axis=0, keepdims=True)


def _stage_softmax(streams, slot):
    for st in streams:
        m_prev = st.m[...]
        m_new = jnp.maximum(m_prev, st.cm[slot])
        alpha = jnp.exp2(m_prev - m_new)
        p = jnp.exp2(st.s[slot] - m_new)
        st.l[...] = alpha * st.l[...] + jnp.sum(p, axis=0, keepdims=True)
        st.p[slot] = p.astype(BF16)
        st.al[slot] = alpha
        st.m[...] = m_new


def _stage_values(vt, streams, slot):
    for st in streams:
        st.acc[...] = st.al[slot] * st.acc[...] + _dot(vt(), st.p[slot])


def _attend(qs, kc_ref, vtc_ref, kl_ref, vtl_ref, n_lat, tkc, streams):
    for st in streams:
        st.m[...] = jnp.full(st.m.shape, -jnp.inf, F32)
        st.l[...] = jnp.zeros(st.l.shape, F32)
        st.acc[...] = jnp.zeros(st.acc.shape, F32)
    for q, st in zip(qs, streams):
        _softmax_step(lambda: kc_ref[...], lambda: vtc_ref[...], q, st)
    if not n_lat:
        return

    def keys(j):
        return lambda: kl_ref[pl.ds(pl.multiple_of(j * tkc, tkc), tkc), :]

    def vals(j):
        return lambda: vtl_ref[j]

    if n_lat < 4 or n_lat % 2:
        def plain(j, carry):
            for q, st in zip(qs, streams):
                _softmax_step(keys(j), vals(j), q, st)
            return carry
        lax.fori_loop(0, n_lat, plain, 0)
        return

    _stage_scores(keys(0), qs, streams, 0)
    _stage_scores(keys(1), qs, streams, 1)
    _stage_softmax(streams, 0)

    def body(i, carry):
        j = 2 * i
        _stage_scores(keys(j), qs, streams, 0)
        _stage_softmax(streams, 1)
        _stage_values(vals(j - 2), streams, 0)
        _stage_scores(keys(j + 1), qs, streams, 1)
        _stage_softmax(streams, 0)
        _stage_values(vals(j - 1), streams, 1)
        return carry

    lax.fori_loop(1, n_lat // 2, body, 0)
    _stage_softmax(streams, 1)
    _stage_values(vals(n_lat - 2), streams, 0)
    _stage_values(vals(n_lat - 1), streams, 1)


def _split_streams(scratch, n):
    per = len(scratch) // n
    return [_Stream(*scratch[i * per:(i + 1) * per]) for i in range(n)]


def _mla_attn_kernel(*refs, n_lat, tkc, n_sub, ts):
    if n_lat:
        q_ref, kc_ref, vtc_ref, kl_ref, vtl_ref, o_ref = refs[:6]
        scratch = refs[6:]
    else:
        q_ref, kc_ref, vtc_ref, o_ref = refs[:4]
        kl_ref = vtl_ref = None
        scratch = refs[4:]
    streams = _split_streams(scratch, n_sub)
    qs = [functools.partial(_load_block, q_ref, sub * ts, ts, 0, q_ref.shape[1]) for sub in range(n_sub)]
    _attend(qs, kc_ref, vtc_ref, kl_ref, vtl_ref, n_lat, tkc, streams)
    for sub, st in enumerate(streams):
        o = st.acc[...] / st.l[...]
        o_ref[sub * ts:(sub + 1) * ts, :] = o.T.astype(o_ref.dtype)


def _mla_attn(q, k_ctx, vt_ctx, k_lat, vt_lat):
    B, Tq, _ = q.shape
    H = MLA_HEADS
    Tc = k_ctx.shape[1]
    tq = _pick(Tq, MLA_TQ)
    n_lat = 0 if k_lat is None else vt_lat.shape[2]
    tkc = 0 if k_lat is None else vt_lat.shape[4]
    in_specs = [pl.BlockSpec((None, tq, 256), lambda b, h, i: (b, i, h)),
                pl.BlockSpec((None, Tc, 256), lambda b, h, i: (b, 0, h)),
                pl.BlockSpec((None, None, None, MLA_V, Tc), lambda b, h, i: (b, h, 0, 0, 0))]
    args = [q, k_ctx, vt_ctx]
    if n_lat:
        T = k_lat.shape[1]
        in_specs += [pl.BlockSpec((None, T, 256), lambda b, h, i: (b, 0, h)),
                     pl.BlockSpec((None, None, n_lat, MLA_V, tkc), lambda b, h, i: (b, h, 0, 0, 0))]
        args += [k_lat, vt_lat]
    ts = _pick(tq, ATTN_SUB)
    n_sub = tq // ts
    return pl.pallas_call(
        functools.partial(_mla_attn_kernel, n_lat=n_lat, tkc=tkc, n_sub=n_sub, ts=ts),
        grid=(B, H, Tq // tq), in_specs=in_specs,
        out_specs=pl.BlockSpec((None, tq, MLA_V), lambda b, h, i: (b, i, h)),
        out_shape=jax.ShapeDtypeStruct((B, Tq, H * MLA_V), BF16),
        scratch_shapes=_stream_scratch(ts, tkc, MLA_V) * n_sub,
        compiler_params=_params(3), name="mla_attn")(*args)


def _diff_prep_kernel(uq_ref, uk_ref, uv_ref, c_ref, s_ref, eye_ref, q_ref, k_ref, vt_ref, *, rope):
    uq = uq_ref[...] * (DIFF_SCALE * LOG2E)
    uk = uk_ref[...]
    lane = lax.broadcasted_iota(jnp.int32, (uq.shape[0], LANES), 1)
    if rope:
        c = c_ref[...]
        s = s_ref[...]
    for h in range(DIFF_HEADS):
        qh = uq[:, LANES * h:LANES * (h + 1)]
        kh = uk[:, LANES * h:LANES * (h + 1)]
        if rope:
            qh = _rope128(qh, c, s)
            kh = _rope128(kh, c, s)
        q_ref[:, 256 * h:256 * h + 128] = jnp.where(lane < DIFF_QK, qh, 0.0).astype(BF16)
        q_ref[:, 256 * h + 128:256 * h + 256] = jnp.where(lane < DIFF_QK, 0.0, qh).astype(BF16)
        k_ref[:, LANES * h:LANES * (h + 1)] = kh.astype(BF16)
    vt = _dot_nt(eye_ref[...], uv_ref[...].astype(BF16))
    tm = vt.shape[1]
    vt_ref[...] = vt.reshape(DIFF_HEADS, DIFF_V, tm).astype(BF16)


def _diff_prep(u, rc, rs, eye, tkc, rope):
    B, T, _ = u.shape
    tm = tkc
    nC = T // tm
    H = DIFF_HEADS
    return pl.pallas_call(
        functools.partial(_diff_prep_kernel, rope=rope), grid=(B, nC),
        in_specs=[pl.BlockSpec((None, tm, U_BLK), lambda b, t: (b, t, U_CQ)),
                  pl.BlockSpec((None, tm, U_BLK), lambda b, t: (b, t, U_CK)),
                  pl.BlockSpec((None, tm, U_BLK), lambda b, t: (b, t, U_CV)),
                  pl.BlockSpec((tm, LANES), lambda b, t: (t, 0)),
                  pl.BlockSpec((tm, LANES), lambda b, t: (t, 0)),
                  pl.BlockSpec(eye.shape, lambda b, t: (0, 0))],
        out_specs=[pl.BlockSpec((None, tm, H * 256), lambda b, t: (b, t, 0)),
                   pl.BlockSpec((None, tm, H * LANES), lambda b, t: (b, t, 0)),
                   pl.BlockSpec((None, H, None, DIFF_V, tm), lambda b, t: (b, 0, t, 0, 0))],
        out_shape=[jax.ShapeDtypeStruct((B, T, H * 256), BF16),
                   jax.ShapeDtypeStruct((B, T, H * LANES), BF16),
                   jax.ShapeDtypeStruct((B, H, nC, DIFF_V, tm), BF16)],
        compiler_params=_params(2), name="diff_prep")(u, u, u, rc, rs, eye)


def _diff_attn_kernel(*refs, n_lat, tkc, lam_init, n_sub, ts):
    if n_lat:
        q_ref, kc_ref, vtc_ref, kl_ref, vtl_ref, lv_ref, sub_ref, o_ref = refs[:8]
        scratch = refs[8:]
    else:
        q_ref, kc_ref, vtc_ref, lv_ref, sub_ref, o_ref = refs[:6]
        kl_ref = vtl_ref = None
        scratch = refs[6:]
    streams = _split_streams(scratch, 2 * n_sub)
    qs = []
    for sub in range(n_sub):
        qs += [functools.partial(_load_block, q_ref, sub * ts, ts, 0, LANES),
               functools.partial(_load_block, q_ref, sub * ts, ts, LANES, LANES)]
    _attend(qs, kc_ref, vtc_ref, kl_ref, vtl_ref, n_lat, tkc, streams)
    lv = lv_ref[...]
    lam = (jnp.exp(jnp.sum(lv[0:1] * lv[1:2], axis=-1, keepdims=True))
           - jnp.exp(jnp.sum(lv[2:3] * lv[3:4], axis=-1, keepdims=True)) + lam_init)
    for sub in range(n_sub):
        st0, st1 = streams[2 * sub], streams[2 * sub + 1]
        o = st0.acc[...] / st0.l[...] - lam * (st1.acc[...] / st1.l[...])
        ot = o.T
        o_ref[sub * ts:(sub + 1) * ts, :] = (_rms(ot, sub_ref[...]) * (1.0 - lam_init)).astype(o_ref.dtype)


def _diff_attn(q, k_ctx, vt_ctx, k_lat, vt_lat, lam_vecs, subln, lam_init):
    B, Tq, _ = q.shape
    H = DIFF_HEADS
    Tc = k_ctx.shape[1]
    tq = _pick(Tq, DIFF_TQ)
    n_lat = 0 if k_lat is None else vt_lat.shape[2]
    tkc = 0 if k_lat is None else vt_lat.shape[4]
    in_specs = [pl.BlockSpec((None, tq, 256), lambda b, h, i: (b, i, h)),
                pl.BlockSpec((None, Tc, LANES), lambda b, h, i: (b, 0, h)),
                pl.BlockSpec((None, None, None, DIFF_V, Tc), lambda b, h, i: (b, h, 0, 0, 0))]
    args = [q, k_ctx, vt_ctx]
    if n_lat:
        T = k_lat.shape[1]
        in_specs += [pl.BlockSpec((None, T, LANES), lambda b, h, i: (b, 0, h)),
                     pl.BlockSpec((None, None, n_lat, DIFF_V, tkc), lambda b, h, i: (b, h, 0, 0, 0))]
        args += [k_lat, vt_lat]
    in_specs += [pl.BlockSpec((4, DIFF_QK), lambda b, h, i: (0, 0)),
                 pl.BlockSpec((1, DIFF_V), lambda b, h, i: (0, 0))]
    args += [lam_vecs, subln.reshape(1, DIFF_V)]
    ts = _pick(tq, ATTN_SUB)
    n_sub = tq // ts
    scratch = _stream_scratch(ts, tkc, DIFF_V) * (2 * n_sub)
    return pl.pallas_call(
        functools.partial(_diff_attn_kernel, n_lat=n_lat, tkc=tkc, lam_init=lam_init, n_sub=n_sub, ts=ts),
        grid=(B, H, Tq // tq), in_specs=in_specs,
        out_specs=pl.BlockSpec((None, tq, DIFF_V), lambda b, h, i: (b, i, h)),
        out_shape=jax.ShapeDtypeStruct((B, Tq, H * DIFF_V), BF16),
        scratch_shapes=scratch, compiler_params=_params(3), name="diff_attn")(*args)


def _gelu_tanh(x):
    return 0.5 * x * (1.0 + jnp.tanh(np.sqrt(2.0 / np.pi) * (x + 0.044715 * (x * x * x))))


def _rglru_kernel(*refs, rev, tb, nt, final):
    if final:
        (x_ref, xp_ref, xn_ref, cw_ref, cb_ref, wri_ref, bri_ref, lam_ref, h0_ref, gate_ref, hf_ref,
         o_ref, ext_ref, carry_ref) = refs
    else:
        (x_ref, xp_ref, xn_ref, cw_ref, cb_ref, wri_ref, bri_ref, lam_ref, h0_ref,
         o_ref, ext_ref, carry_ref) = refs
    i = pl.program_id(1)
    blk = (nt - 1 - i) if rev else i

    @pl.when(i == 0)
    def _():
        carry_ref[...] = h0_ref[...]

    ext_ref[0:SUBLANES, :] = jnp.where(blk > 0, xp_ref[...], 0.0)
    ext_ref[SUBLANES:SUBLANES + tb, :] = x_ref[...]
    ext_ref[SUBLANES + tb:2 * SUBLANES + tb, :] = jnp.where(blk < nt - 1, xn_ref[...], 0.0)
    left = RG_CONV // 2
    xc = cb_ref[...]
    for tap in range(RG_CONV):
        off = SUBLANES - left + tap
        xc = xc + cw_ref[tap:tap + 1, :] * ext_ref[off:off + tb, :]
    xcb = xc.astype(BF16)

    rs, gs = [], []
    for kb in range(RG_BLOCKS):
        ri = _dot(xcb[:, RG_BLOCK * kb:RG_BLOCK * (kb + 1)], wri_ref[kb])
        rs.append(ri[:, :RG_BLOCK])
        gs.append(ri[:, RG_BLOCK:])
    bri = bri_ref[...]
    r = _sigmoid(jnp.concatenate(rs, axis=1) + bri[0:1])
    ig = _sigmoid(jnp.concatenate(gs, axis=1) + bri[1:2])
    lam = lam_ref[...]
    softplus_neg = jnp.maximum(-lam, 0.0) + jnp.log(1.0 + jnp.exp(-jnp.abs(lam)))
    log_a = -RG_C * r * softplus_neg
    a = jnp.exp(log_a)
    bv = jnp.sqrt(-jnp.tanh(log_a) * (a * a + 1.0)) * (ig * xc)

    row = lax.broadcasted_iota(jnp.int32, a.shape, 0)
    d = 1
    while d < tb:
        if rev:
            keep = row < tb - d
            a_sh = pltpu.roll(a, tb - d, 0)
            b_sh = pltpu.roll(bv, tb - d, 0)
        else:
            keep = row >= d
            a_sh = pltpu.roll(a, d, 0)
            b_sh = pltpu.roll(bv, d, 0)
        bv = a * jnp.where(keep, b_sh, 0.0) + bv
        a = a * jnp.where(keep, a_sh, 1.0)
        d *= 2
    h = a * carry_ref[...] + bv
    last = 0 if rev else tb - 1
    carry_ref[...] = h[last:last + 1, :]
    if final:
        o_ref[...] = ((hf_ref[...] + h) * _gelu_tanh(gate_ref[...])).astype(o_ref.dtype)
    else:
        o_ref[...] = h


def _rglru_pass(u, conv_w, conv_b, wri, bri, lam, h0, hf, rev):
    B, T, _ = u.shape
    W = RG_WIDTH
    tb = _pick(T, 256)
    nt = T // tb
    r8 = tb // SUBLANES
    final = hf is not None

    def tix(i):
        return (nt - 1 - i) if rev else i

    in_specs = [pl.BlockSpec((None, tb, U_BLK), lambda b, i: (b, tix(i), U_BX)),
                pl.BlockSpec((None, SUBLANES, U_BLK),
                             lambda b, i: (b, jnp.maximum(tix(i) * r8 - 1, 0), U_BX)),
                pl.BlockSpec((None, SUBLANES, U_BLK),
                             lambda b, i: (b, jnp.minimum((tix(i) + 1) * r8, T // SUBLANES - 1), U_BX)),
                pl.BlockSpec((RG_CONV, W), lambda b, i: (0, 0)),
                pl.BlockSpec((1, W), lambda b, i: (0, 0)),
                pl.BlockSpec((RG_BLOCKS, RG_BLOCK, 2 * RG_BLOCK), lambda b, i: (0, 0, 0)),
                pl.BlockSpec((2, W), lambda b, i: (0, 0)),
                pl.BlockSpec((1, W), lambda b, i: (0, 0)),
                pl.BlockSpec((None, 1, W), lambda b, i: (b, 0, 0))]
    args = [u, u, u, conv_w, conv_b.reshape(1, W), wri, bri, lam.reshape(1, W), h0]
    if final:
        in_specs += [pl.BlockSpec((None, tb, U_BLK), lambda b, i: (b, tix(i), U_BG)),
                     pl.BlockSpec((None, tb, W), lambda b, i: (b, tix(i), 0))]
        args += [u, hf]
    return pl.pallas_call(
        functools.partial(_rglru_kernel, rev=rev, tb=tb, nt=nt, final=final), grid=(B, nt),
        in_specs=in_specs,
        out_specs=pl.BlockSpec((None, tb, W), lambda b, i: (b, tix(i), 0)),
        out_shape=jax.ShapeDtypeStruct((B, T, W), BF16 if final else F32),
        scratch_shapes=[pltpu.VMEM((tb + 2 * SUBLANES, W), F32), pltpu.VMEM((1, W), F32)],
        compiler_params=_params(2), name="rglru_bwd" if rev else "rglru_fwd")(*args)


HGRN_CHUNK = 128
HGRN_PAIR = 2


def _hgrn_chunk(qv, z, v, lb, st, ones, rev):
    C = HGRN_CHUNK
    sg = _sigmoid(z)
    k = (1.0 - lb) * _sigmoid(-z)
    g = jnp.log(lb + (1.0 - lb) * sg)
    row = lax.broadcasted_iota(jnp.int32, (C, LANES), 0)
    col = lax.broadcasted_iota(jnp.int32, (C, C), 1)
    rowc = lax.broadcasted_iota(jnp.int32, (C, C), 0)

    G = g
    d = 1
    while d < C:
        if rev:
            G = G + jnp.where(row < C - d, pltpu.roll(G, C - d, 0), 0.0)
        else:
            G = G + jnp.where(row >= d, pltpu.roll(G, d, 0), 0.0)
        d *= 2
    G = G * LOG2E
    last = 0 if rev else C - 1
    g_last = G[last:last + 1, :]

    qe = (qv * jnp.exp2(G)).astype(BF16)
    o = _dot_nt(qe, st.astype(BF16))
    kdec = (k * jnp.exp2(g_last - G)).astype(BF16)
    vb = v.astype(BF16)
    st_new = st * jnp.exp2(g_last) + _dot(v.T.astype(BF16), kdec)

    pos = (C - 1 - row) if rev else row
    a_sum = jnp.zeros((C, C), F32)
    hsz = 1
    while hsz < C:
        grp = 2 * hsz
        if grp >= SUBLANES:
            ng = C // grp
            G3 = G.reshape(ng, grp, LANES)
            ref = G3[:, hsz:hsz + 1, :] if rev else G3[:, hsz - 1:hsz, :]
            ref = jnp.broadcast_to(ref, (ng, grp, LANES)).reshape(C, LANES)
        else:
            rel = (pos & (grp - 1)) - (hsz - 1)
            ref = G
            for off in range(-(hsz - 1), hsz + 1):
                if off:
                    shift = (-off if rev else off) % C
                    ref = jnp.where(rel == off, pltpu.roll(G, shift, 0), ref)
        e = jnp.exp2(-jnp.abs(G - ref))
        later = ((pos >> int(np.log2(hsz))) & 1) == 1
        ql = jnp.where(later, qv * e, 0.0).astype(BF16)
        kl = jnp.where(later, 0.0, k * e).astype(BF16)
        al = _dot_nt(ql, kl)
        shift = int(np.log2(grp))
        a_sum = a_sum + jnp.where((rowc >> shift) == (col >> shift), al, 0.0)
        hsz = grp
    a_sum = a_sum + jnp.where(rowc == col, _dot((qv * k).astype(BF16), ones), 0.0)
    return o + _dot(a_sum.astype(BF16), vb), st_new


def _hgrn_kernel(*refs, rev, tb, nt, final):
    if final:
        (q_ref, z_ref, v_ref, lb_ref, s0_ref, ones_ref, of_ref, og_ref, gn_ref,
         o_ref, sfin_ref, st_ref) = refs
    else:
        q_ref, z_ref, v_ref, lb_ref, s0_ref, ones_ref, o_ref, sfin_ref, st_ref = refs
    i = pl.program_id(2)

    @pl.when(i == 0)
    def _():
        st_ref[...] = s0_ref[...]

    ones = ones_ref[...]
    C = HGRN_CHUNK
    nch = tb // C

    def body(c, carry):
        cc = (nch - 1 - c) if rev else c
        r0 = pl.multiple_of(cc * C, C)
        for hh in range(HGRN_PAIR):
            lanes = slice(hh * LANES, (hh + 1) * LANES)
            qin = q_ref[pl.ds(r0, C), lanes]
            qv = qin * _sigmoid(qin)
            o, st_new = _hgrn_chunk(qv, z_ref[pl.ds(r0, C), lanes], v_ref[pl.ds(r0, C), lanes],
                                    lb_ref[:, lanes], st_ref[hh], ones, rev)
            st_ref[hh] = st_new
            if final:
                o = o + of_ref[pl.ds(r0, C), lanes]
                og = og_ref[pl.ds(r0, C), lanes]
                o = _rms(o, gn_ref[...]) * (og * _sigmoid(og))
            o_ref[pl.ds(r0, C), lanes] = o.astype(o_ref.dtype)
        return carry

    lax.fori_loop(0, nch, body, 0, unroll=True)

    @pl.when(i == nt - 1)
    def _():
        sfin_ref[...] = st_ref[...]


def _hgrn_pass(u, z_blk, lb, s0, ones, of, g_norm, rev):
    B, T, _ = u.shape
    H = HGRN_HEADS
    tb = _pick(T, 512)
    nt = T // tb
    final = of is not None

    def tix(i):
        return (nt - 1 - i) if rev else i

    P = HGRN_PAIR
    W = P * LANES

    def ublk(blk):
        return pl.BlockSpec((None, tb, W), lambda b, h, i: (b, tix(i), blk * (U_BLK // W) + h))

    in_specs = [ublk(U_DQ), ublk(z_blk), ublk(U_DI),
                pl.BlockSpec((1, W), lambda b, h, i: (0, h)),
                pl.BlockSpec((None, P, HGRN_V, HGRN_K), lambda b, h, i: (b, h, 0, 0)),
                pl.BlockSpec((LANES, LANES), lambda b, h, i: (0, 0))]
    args = [u, u, u, lb.reshape(1, -1), s0, ones]
    if final:
        in_specs += [pl.BlockSpec((None, tb, W), lambda b, h, i: (b, tix(i), h)),
                     ublk(U_DG),
                     pl.BlockSpec((1, HGRN_V), lambda b, h, i: (0, 0))]
        args += [of, u, g_norm.reshape(1, HGRN_V)]
    return pl.pallas_call(
        functools.partial(_hgrn_kernel, rev=rev, tb=tb, nt=nt, final=final), grid=(B, H // P, nt),
        in_specs=in_specs,
        out_specs=[pl.BlockSpec((None, tb, W), lambda b, h, i: (b, tix(i), h)),
                   pl.BlockSpec((None, P, HGRN_V, HGRN_K), lambda b, h, i: (b, h, 0, 0))],
        out_shape=[jax.ShapeDtypeStruct((B, T, H * HGRN_V), BF16 if final else F32),
                   jax.ShapeDtypeStruct((B, H, HGRN_V, HGRN_K), F32)],
        scratch_shapes=[pltpu.VMEM((P, HGRN_V, HGRN_K), F32)],
        compiler_params=_params(3), name="hgrn_bwd" if rev else "hgrn_fwd")(*args)


def _merge_kernel(x_ref, mod_ref, pre_ref, post_ref, ya_ref, yb_ref, yc_ref, yd_ref, w_ref, wb_ref, mb_ref,
                  o_ref, h_ref, acc_ref):
    n = pl.program_id(2)

    @pl.when(n == 0)
    def _():
        h_ref[...] = _prenorm(x_ref[...], mod_ref, 1, pre_ref[...]).astype(BF16)
        acc_ref[...] = jnp.zeros(acc_ref.shape, F32)

    for nb, y_ref in enumerate((ya_ref, yb_ref, yc_ref, yd_ref)):
        @pl.when(n == nb)
        def _(y_ref=y_ref):
            gate = _sigmoid(_dot(h_ref[...], w_ref[...]) + mb_ref[...])
            acc_ref[...] += gate * _dot(y_ref[...], wb_ref[...])

    @pl.when(n == N_BRANCH)
    def _():
        mix = _dot(acc_ref[...].astype(BF16), w_ref[...])
        o_ref[...] = x_ref[...] + mod_ref[5:6, :] * _rms(mix, post_ref[...])


def _merge(x, mod, pre_g, post_g, ys, w5, wb, mb, l):
    B, T, D = x.shape
    tm = _pick(T, 512)
    last = N_BRANCH - 1
    yspec = pl.BlockSpec((None, tm, BRANCH_W), lambda b, t, n: (b, t, 0))
    return pl.pallas_call(
        _merge_kernel, grid=(B, T // tm, N_BRANCH + 1),
        in_specs=[pl.BlockSpec((None, tm, D), lambda b, t, n: (b, t, 0)),
                  pl.BlockSpec((None, N_MOD, D), lambda b, t, n: (b, 0, 0)),
                  pl.BlockSpec((1, D), lambda b, t, n: (0, 0)),
                  pl.BlockSpec((1, D), lambda b, t, n: (0, 0)),
                  yspec, yspec, yspec, yspec,
                  pl.BlockSpec((D, D), lambda b, t, n: (0, n)),
                  pl.BlockSpec((None, None, BRANCH_W, D), lambda b, t, n: (l, jnp.minimum(n, last), 0, 0)),
                  pl.BlockSpec((None, None, 1, D), lambda b, t, n: (l, jnp.minimum(n, last), 0, 0))],
        out_specs=pl.BlockSpec((None, tm, D), lambda b, t, n: (b, t, 0)),
        out_shape=jax.ShapeDtypeStruct(x.shape, F32),
        scratch_shapes=[pltpu.VMEM((tm, D), BF16), pltpu.VMEM((tm, D), F32)],
        compiler_params=_params(3), name="merge")(
            x, mod, pre_g.reshape(1, D), post_g.reshape(1, D), *ys, w5, wb, mb)


def _rope_tables(n_tok):
    axis_dim = MLA_ROPE // 2
    inv_freq = ROPE_BASE ** (-jnp.arange(0, axis_dim, 2, dtype=F32) / axis_dim)
    tok = lax.broadcasted_iota(jnp.int32, (n_tok, LANES), 0)
    lane = lax.broadcasted_iota(jnp.int32, (n_tok, LANES), 1)
    pos = jnp.where((lane & 32) == 0, tok // GRID_W, tok % GRID_W).astype(F32)
    ang = pos * jnp.tile(inv_freq, LANES // inv_freq.shape[0])[None, :]
    cos = jnp.cos(ang)
    sin = jnp.where((lane & 16) == 0, -jnp.sin(ang), jnp.sin(ang))
    first = lane < MLA_ROPE
    return cos, sin, jnp.where(first, cos, 1.0), jnp.where(first, sin, 0.0)


def _layer_weights(l, w_in, mla_w_uq, mla_w_ukv, rg_w_r, rg_w_i, rg_b_r, rg_b_i, w_out):
    D = D_MODEL
    cuts = np.cumsum([MLA_Q_LORA, MLA_KV_LORA, MLA_ROPE, RG_WIDTH, RG_WIDTH, 512, 512, 512,
                      512, 512, 512, 512, 512])
    wi = w_in[l]
    parts = jnp.split(wi[:, :cuts[-1]], cuts[:-1].tolist(), axis=1)
    (a_q, a_kv, a_kr, b_x, b_g, c_q, c_k, c_v, d_q, d_ff, d_fb, d_i, d_g) = parts
    pad = jnp.zeros((D, U_BLK - MLA_KV_LORA - MLA_ROPE), wi.dtype)
    w_u = jnp.concatenate([a_q, b_x, b_g, c_q, c_k, c_v, d_q, d_ff, d_fb, d_i, d_g, a_kv, a_kr, pad],
                          axis=1).astype(BF16)
    w5 = jnp.concatenate([wi[:, cuts[-1]:], w_out[l]], axis=1).astype(BF16)

    wq = mla_w_uq[l].reshape(MLA_Q_LORA, MLA_HEADS, MLA_NOPE + MLA_ROPE)
    wq = jnp.pad(wq, ((0, 0), (0, 0), (0, 256 - MLA_NOPE - MLA_ROPE))).reshape(MLA_Q_LORA, -1).astype(BF16)
    wkv = mla_w_ukv[l].reshape(MLA_KV_LORA, MLA_HEADS, MLA_NOPE + MLA_V)
    wk = wkv[:, :, :MLA_NOPE].reshape(MLA_KV_LORA, -1).astype(BF16)
    wvt = wkv[:, :, MLA_NOPE:].reshape(MLA_KV_LORA, -1).T.astype(BF16)

    wri = [jnp.concatenate([rg_w_r[l, d], rg_w_i[l, d]], axis=-1).astype(BF16) for d in range(2)]
    bri = [jnp.stack([rg_b_r[l, d], rg_b_i[l, d]], axis=0) for d in range(2)]
    return w_u, w5, wq, wk, wvt, wri, bri


def kernel(x, c, ctx, c_ctx, ada_w, ada_b, pre_norm, post_norm, ffn_w_gate, ffn_w_up, ffn_w_down, w_in, mla_q_norm, mla_w_uq, mla_kv_norm, mla_w_ukv, rg_conv_w, rg_conv_b, rg_w_r, rg_b_r, rg_w_i, rg_b_i, rg_lambda, diff_lambda, diff_subln, hgrn_lb_logits, hgrn_norm, merge_b, w_branch, w_out):
    B, S, D = x.shape
    Tc = ctx.shape[1]
    depth = ada_w.shape[0]
    tkc_mla = _pick(S, MLA_TKC)
    tkc_diff = _pick(S, DIFF_TKC)

    rc_diff, rs_diff, rc_mla, rs_mla = _rope_tables(S)
    rc_ctx = jnp.ones((Tc, LANES), F32)
    rs_ctx = jnp.zeros((Tc, LANES), F32)

    lb_soft = jax.nn.softmax(hgrn_lb_logits.astype(F32), axis=0)
    hgrn_lb = jnp.cumsum(lb_soft, axis=0) - lb_soft[0]

    c8 = jnp.zeros((SUBLANES, D), F32).at[:B].set(c).at[B].set(c_ctx)
    eye = jnp.eye(DIFF_HEADS * DIFF_V, dtype=BF16)
    ones = jnp.ones((LANES, LANES), BF16)
    zero_state = jnp.zeros((B, HGRN_HEADS, HGRN_V, HGRN_K), F32)
    zero_h = jnp.zeros((B, 1, RG_WIDTH), F32)

    wg = ffn_w_gate.astype(BF16)
    wu = ffn_w_up.astype(BF16)
    wd = ffn_w_down.astype(BF16)
    wb = w_branch.astype(BF16)
    mb = merge_b.reshape(depth, N_BRANCH, 1, D)

    xc = ctx.reshape(1, B * Tc, D)
    for l in range(depth):
        need_ctx = l < depth - 1
        lam_init = 0.8 - 0.6 * float(np.exp(-0.3 * l))
        mod8 = _ada_mod(c8, ada_w, ada_b, l).reshape(SUBLANES, N_MOD, D)
        mod = mod8[:B]
        mod_c = mod8[B:B + 1]
        w_u, w5, wq, wk, wvt, wri, bri = _layer_weights(
            l, w_in, mla_w_uq, mla_w_ukv, rg_w_r, rg_w_i, rg_b_r, rg_b_i, w_out)

        x = _ffn(x, mod, 0, pre_norm[l, 0], post_norm[l, 0], wg, wu, wd, l, 0)
        xc = _ffn(xc, mod_c, 0, pre_norm[l, 0], post_norm[l, 0], wg, wu, wd, l, 0)

        u = _inproj(x, mod, pre_norm[l, 1], w_u)
        uc = _inproj(xc, mod_c, pre_norm[l, 1], w_u).reshape(B, Tc, -1)

        qa_c, ka_c, vta_c = _mla_prep(uc, rc_ctx, rs_ctx, mla_q_norm[l], mla_kv_norm[l], wq, wk, wvt, Tc, False)
        qa, ka, vta = _mla_prep(u, rc_mla, rs_mla, mla_q_norm[l], mla_kv_norm[l], wq, wk, wvt, tkc_mla, True)
        y_a = _mla_attn(qa, ka_c, vta_c, ka, vta)

        qd_c, kd_c, vtd_c = _diff_prep(uc, rc_ctx, rs_ctx, eye, Tc, False)
        qd, kd, vtd = _diff_prep(u, rc_diff, rs_diff, eye, tkc_diff, True)
        y_c = _diff_attn(qd, kd_c, vtd_c, kd, vtd, diff_lambda[l], diff_subln[l], lam_init)

        hc_f = _rglru_pass(uc, rg_conv_w[l], rg_conv_b[l], wri[0], bri[0], rg_lambda[l, 0], zero_h, None, False)
        hl_f = _rglru_pass(u, rg_conv_w[l], rg_conv_b[l], wri[0], bri[0], rg_lambda[l, 0],
                           hc_f[:, Tc - 1:Tc, :], None, False)
        if need_ctx:
            y_b_c = _rglru_pass(uc, rg_conv_w[l], rg_conv_b[l], wri[1], bri[1], rg_lambda[l, 1], zero_h, hc_f, True)
        hc_b = _rglru_pass(uc, rg_conv_w[l], rg_conv_b[l], wri[1], bri[1], rg_lambda[l, 1], zero_h, None, True)
        y_b = _rglru_pass(u, rg_conv_w[l], rg_conv_b[l], wri[1], bri[1], rg_lambda[l, 1],
                          hc_b[:, 0:1, :], hl_f, True)

        oc_f, s_f = _hgrn_pass(uc, U_DFF, hgrn_lb[l], zero_state, ones, None, None, False)
        if need_ctx:
            y_d_c, s_b = _hgrn_pass(uc, U_DFB, hgrn_lb[l], zero_state, ones, oc_f, hgrn_norm[l], True)
        else:
            _, s_b = _hgrn_pass(uc, U_DFB, hgrn_lb[l], zero_state, ones, None, None, True)
        o_f, _ = _hgrn_pass(u, U_DFF, hgrn_lb[l], s_f, ones, None, None, False)
        y_d, _ = _hgrn_pass(u, U_DFB, hgrn_lb[l], s_b, ones, o_f, hgrn_norm[l], True)

        x = _merge(x, mod, pre_norm[l, 1], post_norm[l, 1], (y_a, y_b, y_c, y_d), w5, wb, mb, l)
        if need_ctx:
            y_a_c = _mla_attn(qa_c, ka_c, vta_c, None, None)
            y_c_c = _diff_attn(qd_c, kd_c, vtd_c, None, None, diff_lambda[l], diff_subln[l], lam_init)
            ys_c = tuple(y.reshape(1, B * Tc, BRANCH_W) for y in (y_a_c, y_b_c, y_c_c, y_d_c))
            xc = _merge(xc, mod_c, pre_norm[l, 1], post_norm[l, 1], ys_c, w5, wb, mb, l)
            xc = _ffn(xc, mod_c, 2, pre_norm[l, 2], post_norm[l, 2], wg, wu, wd, l, 1)

        x = _ffn(x, mod, 2, pre_norm[l, 2], post_norm[l, 2], wg, wu, wd, l, 1)
    return x
```
